```python
import math
import jax, jax.numpy as jnp
from jax import lax
import numpy as np

D_MODEL = 1024
BATCH = 1
SEQ = 16384
DEPTH = 2
DEC_BATCH = 4
DEC_SEQ = 8192
PAST_LEN = 128

BRANCH_W = 512
N_BRANCH = 4
POOL_WINDOWS = (2, 4, 8, 16)
POOL_GROUPS = 4
POOL_GW = BRANCH_W // POOL_GROUPS
DA_HEADS = 4
DA_HEAD_DIM = 64
DA_V_DIM = 2 * DA_HEAD_DIM
Q_BLOCK = 128
REL_BUCKETS = 32
REL_MAX_DIST = 128
HY_SHORT = 3
HY_EMB = 33
HY_BANDS = (HY_EMB - 1) // 2
HY_FILTER_ORDER = 64
HY_DECAY_TARGET = 1e-2
HY_FAST = 0.3
HY_SLOW = 1.5
CF_WIDTH = 31
EPS = 1e-6

OFF_POOL = 0
OFF_Q = OFF_POOL + BRANCH_W
OFF_K = OFF_Q + DA_HEADS * 2 * DA_HEAD_DIM
OFF_V = OFF_K + DA_HEADS * 2 * DA_HEAD_DIM
OFF_HY = OFF_V + DA_HEADS * DA_V_DIM
OFF_CF = OFF_HY + 3 * BRANCH_W
OFF_SILU = OFF_CF + 2 * BRANCH_W
OFF_MERGE = OFF_SILU + N_BRANCH * BRANCH_W
N_IN = OFF_MERGE + N_BRANCH * D_MODEL

kernel_name = 'hybrid_pool_diffattn_hyena_conformer_encoder'

F32 = jnp.float32


def rmsnorm(x, g):
    xf = x.astype(F32)
    y = xf * lax.rsqrt(jnp.mean(xf * xf, axis=-1, keepdims=True) + EPS)
    return (y * g.astype(F32)).astype(x.dtype)


def layernorm(x, g, b):
    xf = x.astype(F32)
    mu = jnp.mean(xf, axis=-1, keepdims=True)
    var = jnp.mean(jnp.square(xf - mu), axis=-1, keepdims=True)
    y = (xf - mu) * lax.rsqrt(var + EPS)
    return (y * g.astype(F32) + b.astype(F32)).astype(x.dtype)


def depthwise_conv_centred(u, w, b):
    width = w.shape[0]
    out = lax.conv_general_dilated(u, w[:, None, :].astype(u.dtype), window_strides=(1,),
                                   padding=[(width // 2, width // 2)],
                                   dimension_numbers=('NWC', 'WIO', 'NWC'),
                                   feature_group_count=u.shape[-1])
    return out + b.astype(u.dtype)


def pool_mixer(u, w_mix, scale):
    b, L, c = u.shape
    uf = u.astype(F32)
    cs = jnp.concatenate([jnp.zeros((b, 1, c), F32), jnp.cumsum(uf, axis=1)], axis=1)
    pos = jnp.arange(L)
    outs = []
    for g, w in enumerate(POOL_WINDOWS):
        sl = slice(g * POOL_GW, (g + 1) * POOL_GW)
        lo = jnp.clip(pos - w // 2, 0, L - 1)
        hi = jnp.clip(pos + (w - 1 - w // 2), 0, L - 1)
        csg = cs[:, :, sl]
        s = jnp.take(csg, hi + 1, axis=1) - jnp.take(csg, lo, axis=1)
        cnt = (hi - lo + 1).astype(F32)[None, :, None]
        outs.append(s / cnt - uf[:, :, sl])
    pooled = jnp.stack(outs, axis=2)
    mixed = jnp.einsum('blgc,gcd->blgd', pooled, w_mix.astype(F32)).reshape(b, L, c)
    return (mixed * scale.astype(F32)).astype(u.dtype)


def rel_bucket(rel):
    nb = REL_BUCKETS // 2
    max_exact = nb // 2
    ret = jnp.where(rel > 0, nb, 0)
    n = jnp.abs(rel)
    large = max_exact + (jnp.log(jnp.maximum(n, 1).astype(F32) / max_exact)
                         / math.log(REL_MAX_DIST / max_exact) * (nb - max_exact)).astype(jnp.int32)
    large = jnp.minimum(large, nb - 1)
    return ret + jnp.where(n < max_exact, n, large)


def diff_attention(q, k, v, lam, rel_bias):
    b, L = q.shape[0], q.shape[1]
    nblk = L // Q_BLOCK
    scale = DA_HEAD_DIM ** -0.5
    kpos = jnp.arange(L)
    table = rel_bias.astype(F32)
    qb = q.reshape(b, nblk, Q_BLOCK, DA_HEADS, 2, DA_HEAD_DIM).transpose(1, 0, 2, 3, 4, 5)

    def block(args):
        q_blk, start = args
        s = jnp.einsum('bqhmd,bkhmd->bhmqk', q_blk, k, preferred_element_type=F32) * scale
        qpos = start + jnp.arange(Q_BLOCK)
        bias = table[rel_bucket(kpos[None, :] - qpos[:, None])]
        s = s + bias.transpose(2, 0, 1)[None, :, None]
        p = jax.nn.softmax(s, axis=-1)
        a = p[:, :, 0] - lam * p[:, :, 1]
        return jnp.einsum('bhqk,bkhe->bqhe', a.astype(v.dtype), v)

    starts = jnp.arange(nblk, dtype=jnp.int32) * Q_BLOCK
    out = lax.map(block, (qb, starts))
    return out.transpose(1, 0, 2, 3, 4).reshape(b, L, DA_HEADS, DA_V_DIM)


def hyena_filter_freq(L, w1, b1, freq, w2, b2, w3):
    t = jnp.linspace(0.0, 1.0, L, dtype=F32)[:, None]
    w = 2.0 * math.pi * jnp.arange(L, dtype=F32)[:, None] / L
    bands = jnp.linspace(1e-4, HY_BANDS - 1, HY_BANDS, dtype=F32)[None, :]
    emb = jnp.concatenate([t, jnp.cos(bands * w), -jnp.sin(bands * w)], axis=-1)
    freq = freq.astype(F32)
    h = jnp.sin(freq[0] * (emb @ w1.astype(F32) + b1.astype(F32)))
    h = jnp.sin(freq[1] * (h @ w2.astype(F32) + b2.astype(F32)))
    h = (h @ w3.astype(F32)).reshape(L, 2, BRANCH_W)
    max_decay = math.log(HY_DECAY_TARGET) / HY_FAST
    min_decay = math.log(HY_DECAY_TARGET) / HY_SLOW
    deltas = jnp.linspace(min_decay, max_decay, BRANCH_W, dtype=F32)
    decay = jnp.exp(-t * jnp.abs(deltas)[None, :])
    h = h * decay[:, None, :]
    h = h / (jnp.sum(jnp.abs(h), axis=(0, 1), keepdims=True) + EPS)
    filt = jnp.concatenate([h[:, 0], jnp.zeros((1, BRANCH_W), F32), h[:0:-1, 1]], axis=0)
    return jnp.fft.rfft(filt, axis=0)


def hyena_mixer(u, short_w, short_b, filt_f, d_bias):
    L = u.shape[1]
    uc = depthwise_conv_centred(u, short_w, short_b)
    x0, x1, v = jnp.split(uc, 3, axis=-1)
    z = (x1 * v).astype(F32)
    zf = jnp.fft.rfft(z, n=2 * L, axis=1)
    y = jnp.fft.irfft(zf * filt_f[None], n=2 * L, axis=1)[:, :L] + z * d_bias.astype(F32)
    return (x0.astype(F32) * y).astype(u.dtype)


def conformer_conv(u, dw_w, dw_b, ln_g, ln_b):
    a, g = jnp.split(u, 2, axis=-1)
    h = a * jax.nn.sigmoid(g)
    h = depthwise_conv_centred(h, dw_w, dw_b)
    h = layernorm(h, ln_g, ln_b)
    return jax.nn.silu(h)


def hybrid_layer(x, layer_idx, filt_f, norm_g, w_in, pool_w, pool_scale, lq1, lk1, lq2, lk2, subln_g,
                 rel_bias, short_w, short_b, hy_d, dw_w, dw_b, ln_g, ln_b, w_branch, w_out):
    b, L, _ = x.shape
    h = rmsnorm(x, norm_g)
    p = h @ w_in
    ya = pool_mixer(p[..., OFF_POOL:OFF_Q], pool_w, pool_scale)
    q = p[..., OFF_Q:OFF_K].reshape(b, L, DA_HEADS, 2, DA_HEAD_DIM)
    k = p[..., OFF_K:OFF_V].reshape(b, L, DA_HEADS, 2, DA_HEAD_DIM)
    v = p[..., OFF_V:OFF_HY].reshape(b, L, DA_HEADS, DA_V_DIM)
    lam_init = 0.8 - 0.6 * math.exp(-0.3 * layer_idx)
    lam = (jnp.exp(jnp.sum(lq1.astype(F32) * lk1.astype(F32)))
           - jnp.exp(jnp.sum(lq2.astype(F32) * lk2.astype(F32))) + lam_init)
    o = diff_attention(q, k, v, lam, rel_bias)
    yb = (rmsnorm(o, subln_g) * (1.0 - lam_init)).reshape(b, L, BRANCH_W)
    yc = hyena_mixer(p[..., OFF_HY:OFF_CF], short_w, short_b, filt_f, hy_d)
    yd = conformer_conv(p[..., OFF_CF:OFF_SILU], dw_w, dw_b, ln_g, ln_b)
    branches = jnp.stack([ya, yb, yc, yd], axis=2)
    silu_gate = jax.nn.silu(p[..., OFF_SILU:OFF_MERGE].reshape(b, L, N_BRANCH, BRANCH_W))
    proj = jnp.einsum('blnw,nwd->blnd', branches * silu_gate, w_branch)
    merge = jax.nn.sigmoid(p[..., OFF_MERGE:].reshape(b, L, N_BRANCH, D_MODEL))
    mixed = jnp.sum(merge * proj, axis=2)
    return x + mixed @ w_out


def setup_inputs(seed: int = 0) -> dict:
    key = jax.random.key(seed)
    ks = jax.random.split(key, 28)
    W = BRANCH_W

    def nrm(k, shape, s):
        return jax.random.normal(k, shape, F32) * s

    return {
        'x_prompt': nrm(ks[0], (BATCH, SEQ, D_MODEL), 1.0),
        'x_sample': nrm(ks[1], (DEC_BATCH, DEC_SEQ, D_MODEL), 1.0),
        'norm_g': 1.0 + nrm(ks[2], (DEPTH, D_MODEL), 0.02),
        'w_in': nrm(ks[3], (DEPTH, D_MODEL, N_IN), D_MODEL ** -0.5),
        'pool_w': nrm(ks[4], (DEPTH, POOL_GROUPS, POOL_GW, POOL_GW), POOL_GW ** -0.5),
        'pool_scale': 1.0 + nrm(ks[5], (DEPTH, W), 0.1),
        'lambda_q1': nrm(ks[6], (DEPTH, DA_HEAD_DIM), 0.1),
        'lambda_k1': nrm(ks[7], (DEPTH, DA_HEAD_DIM), 0.1),
        'lambda_q2': nrm(ks[8], (DEPTH, DA_HEAD_DIM), 0.1),
        'lambda_k2': nrm(ks[9], (DEPTH, DA_HEAD_DIM), 0.1),
        'subln_g': 1.0 + nrm(ks[10], (DEPTH, DA_V_DIM), 0.02),
        'rel_bias': nrm(ks[11], (REL_BUCKETS, DA_HEADS), 0.5),
        'hy_short_w': nrm(ks[12], (DEPTH, HY_SHORT, 3 * W), HY_SHORT ** -0.5),
        'hy_short_b': nrm(ks[13], (DEPTH, 3 * W), 0.02),
        'hy_w1': nrm(ks[14], (DEPTH, HY_EMB, HY_FILTER_ORDER), HY_EMB ** -0.5),
        'hy_b1': nrm(ks[15], (DEPTH, HY_FILTER_ORDER), 0.02),
        'hy_freq': 1.0 + nrm(ks[16], (DEPTH, 2, HY_FILTER_ORDER), 0.02),
        'hy_w2': nrm(ks[17], (DEPTH, HY_FILTER_ORDER, HY_FILTER_ORDER), HY_FILTER_ORDER ** -0.5),
        'hy_b2': nrm(ks[18], (DEPTH, HY_FILTER_ORDER), 0.02),
        'hy_w3': nrm(ks[19], (DEPTH, HY_FILTER_ORDER, 2 * W), HY_FILTER_ORDER ** -0.5),
        'hy_d': nrm(ks[20], (DEPTH, W), 0.5),
        'cf_dw_w': nrm(ks[21], (DEPTH, CF_WIDTH, W), CF_WIDTH ** -0.5),
        'cf_dw_b': nrm(ks[22], (DEPTH, W), 0.02),
        'cf_ln_g': 1.0 + nrm(ks[23], (DEPTH, W), 0.02),
        'cf_ln_b': nrm(ks[24], (DEPTH, W), 0.02),
        'w_branch': nrm(ks[25], (DEPTH, N_BRANCH, W, D_MODEL), W ** -0.5),
        'w_out': nrm(ks[26], (DEPTH, D_MODEL, D_MODEL), D_MODEL ** -0.5),
        'final_g': 1.0 + nrm(ks[27], (D_MODEL,), 0.02),
    }


def reference(x_prompt, x_sample, norm_g, w_in, pool_w, pool_scale, lambda_q1, lambda_k1, lambda_q2,
              lambda_k2, subln_g, rel_bias, hy_short_w, hy_short_b, hy_w1, hy_b1, hy_freq, hy_w2, hy_b2,
              hy_w3, hy_d, cf_dw_w, cf_dw_b, cf_ln_g, cf_ln_b, w_branch, w_out, final_g):
    hp = x_prompt
    hs = x_sample
    for l in range(DEPTH):
        filt_p = hyena_filter_freq(hp.shape[1], hy_w1[l], hy_b1[l], hy_freq[l], hy_w2[l], hy_b2[l], hy_w3[l])
        filt_s = hyena_filter_freq(hs.shape[1], hy_w1[l], hy_b1[l], hy_freq[l], hy_w2[l], hy_b2[l], hy_w3[l])
        hp = hybrid_layer(hp, l, filt_p, norm_g[l], w_in[l], pool_w[l], pool_scale[l], lambda_q1[l],
                          lambda_k1[l], lambda_q2[l], lambda_k2[l], subln_g[l], rel_bias, hy_short_w[l],
                          hy_short_b[l], hy_d[l], cf_dw_w[l], cf_dw_b[l], cf_ln_g[l], cf_ln_b[l],
                          w_branch[l], w_out[l])
        hs = hybrid_layer(hs, l, filt_s, norm_g[l], w_in[l], pool_w[l], pool_scale[l], lambda_q1[l],
                          lambda_k1[l], lambda_q2[l], lambda_k2[l], subln_g[l], rel_bias, hy_short_w[l],
                          hy_short_b[l], hy_d[l], cf_dw_w[l], cf_dw_b[l], cf_ln_g[l], cf_ln_b[l],
                          w_branch[l], w_out[l])
    y_prompt = rmsnorm(hp, final_g)
    y_sample = rmsnorm(hs, final_g)
    return (y_prompt, y_sample)
```

```python
import functools
import math

import numpy as np
import jax
import jax.numpy as jnp
from jax import lax
from jax.experimental import pallas as pl
from jax.experimental.pallas import tpu as pltpu

F32 = jnp.float32
BF16 = jnp.bfloat16

D_MODEL = 1024
BRANCH_W = 512
N_BRANCH = 4
POOL_WINDOWS = (2, 4, 8, 16)
POOL_GW = 128
DA_HEADS = 4
DA_HEAD_DIM = 64
DA_V_DIM = 128
REL_BUCKETS = 32
REL_MAX_DIST = 128
HY_EMB = 33
HY_EMB_PAD = 40
HY_BANDS = 16
HY_ORDER = 64
HY_DECAY_TARGET = 1e-2
HY_FAST = 0.3
HY_SLOW = 1.5
CF_WIDTH = 31
EPS = 1e-6

OFF_POOL = 0
OFF_Q = 512
OFF_K = 1024
OFF_V = 1536
OFF_HY = 2048
OFF_CF = 3584
OFF_SILU = 4608
OFF_MERGE = 6656

LANES = 128
FFT_N1 = 256
HALO = 16
TOK_TILE = 512
ATT_TQ = 256
ATT_TK = 512
NEG_BIG = -1e30


def _cparams(sem, vmem_mb):
    return pltpu.CompilerParams(dimension_semantics=sem, vmem_limit_bytes=vmem_mb << 20)


def _proj_kernel(x_ref, g_ref, w_ref, o_ref, h_ref, *, transposed):
    @pl.when(pl.program_id(1) == 0)
    def _():
        x = x_ref[...]
        ms = jnp.mean(x * x, axis=-1, keepdims=True)
        h_ref[...] = (x * lax.rsqrt(ms + EPS) * g_ref[...]).astype(BF16)

    h = h_ref[...]
    if transposed:
        o = lax.dot_general(w_ref[...], h, (((1,), (1,)), ((), ())), preferred_element_type=F32)
    else:
        o = jnp.dot(h, w_ref[...], preferred_element_type=F32)
    o_ref[...] = o.astype(o_ref.dtype)


def _proj(x2, g, w, out_dtype, tn):
    T, D = x2.shape
    N = w.shape[1]
    tm = TOK_TILE
    return pl.pallas_call(
        functools.partial(_proj_kernel, transposed=False),
        grid=(T // tm, N // tn),
        in_specs=[pl.BlockSpec((tm, D), lambda i, j: (i, 0)),
                  pl.BlockSpec((1, D), lambda i, j: (0, 0)),
                  pl.BlockSpec((D, tn), lambda i, j: (0, j))],
        out_specs=pl.BlockSpec((tm, tn), lambda i, j: (i, j)),
        out_shape=jax.ShapeDtypeStruct((T, N), out_dtype),
        scratch_shapes=[pltpu.VMEM((tm, D), BF16)],
        compiler_params=_cparams(("parallel", "arbitrary"), 40),
        name="proj_tok",
    )(x2, g, w)


def _proj_t(x2, g, wt, out_dtype, tn, B, L, head_layout):
    T, D = x2.shape
    N = wt.shape[0]
    tm = TOK_TILE
    nl = L // tm
    if head_layout:
        assert tn == LANES
        out_shape = jax.ShapeDtypeStruct((B, N // tn, nl, tn, tm), out_dtype)
        out_spec = pl.BlockSpec((None, None, None, tn, tm), lambda i, j: (i // nl, j, i % nl, 0, 0))
    else:
        out_shape = jax.ShapeDtypeStruct((B, N, L), out_dtype)
        out_spec = pl.BlockSpec((None, tn, tm), lambda i, j: (i // nl, j, i % nl))
    return pl.pallas_call(
        functools.partial(_proj_kernel, transposed=True),
        grid=(T // tm, N // tn),
        in_specs=[pl.BlockSpec((tm, D), lambda i, j: (i, 0)),
                  pl.BlockSpec((1, D), lambda i, j: (0, 0)),
                  pl.BlockSpec((tn, D), lambda i, j: (j, 0))],
        out_specs=out_spec,
        out_shape=out_shape,
        scratch_shapes=[pltpu.VMEM((tm, D), BF16)],
        compiler_params=_cparams(("parallel", "arbitrary"), 40),
        name="proj_chan",
    )(x2, g, wt)


def _halo_specs(tm, C, T):
    r = tm // HALO
    last = T // HALO - 1
    return [pl.BlockSpec((tm, C), lambda i: (i, 0)),
            pl.BlockSpec((HALO, C), lambda i: (jnp.maximum(i * r - 1, 0), 0)),
            pl.BlockSpec((HALO, C), lambda i: (jnp.minimum((i + 1) * r, last), 0))]


def _fill_ext(ext_ref, cur_ref, prev_ref, next_ref, tm, L):
    i = pl.program_id(0)
    tiles_per_seq = L // tm
    first = (i % tiles_per_seq) == 0
    lastt = (i % tiles_per_seq) == tiles_per_seq - 1
    ext_ref[pl.ds(0, HALO), :] = jnp.where(first, 0.0, prev_ref[...])
    ext_ref[pl.ds(HALO, tm), :] = cur_ref[...]
    ext_ref[pl.ds(HALO + tm, HALO), :] = jnp.where(lastt, 0.0, next_ref[...])


def _pool_kernel(cur_ref, prev_ref, next_ref, w_ref, sc_ref, o_ref, ext_ref, *, tm, L):
    _fill_ext(ext_ref, cur_ref, prev_ref, next_ref, tm, L)
    i = pl.program_id(0)
    pos = (i % (L // tm)) * tm + lax.broadcasted_iota(jnp.int32, (tm, 1), 0)
    for g, w in enumerate(POOL_WINDOWS):
        lanes = pl.ds(g * POOL_GW, POOL_GW)
        s = None
        for d in range(-(w // 2), w - w // 2):
            t = ext_ref[pl.ds(HALO + d, tm), lanes]
            s = t if s is None else s + t
        lo = jnp.maximum(pos - w // 2, 0)
        hi = jnp.minimum(pos + (w - 1 - w // 2), L - 1)
        cnt = (hi - lo + 1).astype(F32)
        pooled = s / cnt - ext_ref[pl.ds(HALO, tm), lanes]
        mixed = jnp.dot(pooled.astype(BF16), w_ref[g], preferred_element_type=F32)
        o_ref[:, lanes] = mixed * sc_ref[:, lanes]


def _pool(u, w_mix, scale, L):
    T, C = u.shape
    tm = TOK_TILE
    return pl.pallas_call(
        functools.partial(_pool_kernel, tm=tm, L=L),
        grid=(T // tm,),
        in_specs=_halo_specs(tm, C, T) + [
            pl.BlockSpec((4, POOL_GW, POOL_GW), lambda i: (0, 0, 0)),
            pl.BlockSpec((1, C), lambda i: (0, 0))],
        out_specs=pl.BlockSpec((tm, C), lambda i: (i, 0)),
        out_shape=jax.ShapeDtypeStruct((T, C), F32),
        scratch_shapes=[pltpu.VMEM((tm + 2 * HALO, C), F32)],
        compiler_params=_cparams(("parallel",), 32),
        name="pool_mixer",
    )(u, u, u, w_mix, scale)


def _conf_kernel(cur_ref, prev_ref, next_ref, w_ref, b_ref, g_ref, beta_ref, o_ref, ext_ref, h_ref, *, tm, L):
    _fill_ext(ext_ref, cur_ref, prev_ref, next_ref, tm, L)
    a = ext_ref[:, pl.ds(0, BRANCH_W)]
    gt = ext_ref[:, pl.ds(BRANCH_W, BRANCH_W)]
    h_ref[...] = a * jax.nn.sigmoid(gt)
    acc = jnp.zeros((tm, BRANCH_W), F32) + b_ref[...]
    for j in range(CF_WIDTH):
        acc = acc + w_ref[pl.ds(j, 1), :] * h_ref[pl.ds(HALO - CF_WIDTH // 2 + j, tm), :]
    mu = jnp.mean(acc, axis=-1, keepdims=True)
    cen = acc - mu
    var = jnp.mean(cen * cen, axis=-1, keepdims=True)
    y = cen * lax.rsqrt(var + EPS) * g_ref[...] + beta_ref[...]
    o_ref[...] = y * jax.nn.sigmoid(y)


def _conformer(u, dw_w, dw_b, ln_g, ln_b, L):
    T, C2 = u.shape
    C = C2 // 2
    tm = TOK_TILE
    vec = pl.BlockSpec((1, C), lambda i: (0, 0))
    return pl.pallas_call(
        functools.partial(_conf_kernel, tm=tm, L=L),
        grid=(T // tm,),
        in_specs=_halo_specs(tm, C2, T) + [pl.BlockSpec((CF_WIDTH, C), lambda i: (0, 0)), vec, vec, vec],
        out_specs=pl.BlockSpec((tm, C), lambda i: (i, 0)),
        out_shape=jax.ShapeDtypeStruct((T, C), F32),
        scratch_shapes=[pltpu.VMEM((tm + 2 * HALO, C2), F32), pltpu.VMEM((tm + 2 * HALO, C), F32)],
        compiler_params=_cparams(("parallel",), 32),
        name="conformer_conv",
    )(u, u, u, dw_w, dw_b, ln_g, ln_b)


def _rel_bucket_np(rel):
    nb = REL_BUCKETS // 2
    max_exact = nb // 2
    ret = np.where(rel > 0, nb, 0)
    n = np.abs(rel)
    ratio = np.log(np.maximum(n, 1).astype(np.float32) / np.float32(max_exact)) / np.float32(
        math.log(REL_MAX_DIST / max_exact))
    large = max_exact + (ratio * np.float32(nb - max_exact)).astype(np.int32)
    large = np.minimum(large, nb - 1)
    return ret + np.where(n < max_exact, n, large)


def _near_bias_buckets():
    r = np.arange(ATT_TK)[:, None]
    c = np.arange(ATT_TQ)[None, :]
    out = np.zeros((2, 3, ATT_TK, ATT_TQ), np.int32)
    for e in range(2):
        for dj in (-1, 0, 1):
            out[e, dj + 1] = _rel_bucket_np(ATT_TK * dj + r - ATT_TQ * e - c)
    return out


def _attn_kernel(cfar_ref, q_ref, k_ref, vt_ref, bias_ref, lq1_ref, lk1_ref, lq2_ref, lk2_ref, sg_ref,
                 o_ref, qm_ref, m_ref, l_ref, acc_ref, *, nk, lam_init):
    h = pl.program_id(1)
    qi = pl.program_id(2)
    q = q_ref[...]
    lane = lax.broadcasted_iota(jnp.int32, q.shape, 1)
    qs = q * jnp.asarray(DA_HEAD_DIM ** -0.5, BF16)
    zero = jnp.zeros_like(qs)
    qm_ref[0] = jnp.where(lane < DA_HEAD_DIM, qs, zero)
    qm_ref[1] = jnp.where(lane >= DA_HEAD_DIM, qs, zero)
    m_ref[...] = jnp.full(m_ref.shape, NEG_BIG, F32)
    l_ref[...] = jnp.zeros(l_ref.shape, F32)
    acc_ref[...] = jnp.zeros(acc_ref.shape, F32)
    j0 = qi // 2
    e = qi % 2

    def step(j, bias):
        kblk = k_ref[pl.ds(pl.multiple_of(j * ATT_TK, ATT_TK), ATT_TK), :]
        vblk = vt_ref[j]
        for mm in range(2):
            s = lax.dot_general(kblk, qm_ref[mm], (((1,), (1,)), ((), ())), preferred_element_type=F32)
            s = s + bias
            m_prev = m_ref[mm]
            m_new = jnp.maximum(m_prev, jnp.max(s, axis=0, keepdims=True))
            alpha = jnp.exp(m_prev - m_new)
            p = jnp.exp(s - m_new)
            l_ref[mm] = alpha * l_ref[mm] + jnp.sum(p, axis=0, keepdims=True)
            acc_ref[mm] = alpha * acc_ref[mm] + jnp.dot(vblk, p.astype(BF16), preferred_element_type=F32)
            m_ref[mm] = m_new

    def far(c):
        def body(j, carry):
            step(j, c)
            return carry
        return body

    lax.fori_loop(0, jnp.maximum(j0 - 1, 0), far(cfar_ref[h, 0]), 0)
    for dj in (-1, 0, 1):
        j = j0 + dj

        @pl.when((j >= 0) & (j < nk))
        def _():
            step(j, bias_ref[e, dj + 1])
    lax.fori_loop(jnp.minimum(j0 + 2, nk), nk, far(cfar_ref[h, 1]), 0)

    lam = (jnp.exp(jnp.sum(lq1_ref[...] * lk1_ref[...], keepdims=True))
           - jnp.exp(jnp.sum(lq2_ref[...] * lk2_ref[...], keepdims=True)) + lam_init)
    o = acc_ref[0] / l_ref[0] - lam * (acc_ref[1] / l_ref[1])
    ms = jnp.mean(o * o, axis=0, keepdims=True)
    y = o * lax.rsqrt(ms + EPS) * sg_ref[...] * (1.0 - lam_init)
    o_ref[...] = y.T


def _attention(qk, vt, bias, cfar, lq1, lk1, lq2, lk2, sg, lam_init):
    B, L, _ = qk.shape
    nk = L // ATT_TK
    vec = pl.BlockSpec((1, DA_HEAD_DIM), lambda b, h, i: (0, 0))
    return pl.pallas_call(
        functools.partial(_attn_kernel, nk=nk, lam_init=lam_init),
        grid=(B, DA_HEADS, L // ATT_TQ),
        in_specs=[pl.BlockSpec(memory_space=pltpu.SMEM),
                  pl.BlockSpec((None, ATT_TQ, LANES), lambda b, h, i: (b, i, h)),
                  pl.BlockSpec((None, L, LANES), lambda b, h, i: (b, 0, DA_HEADS + h)),
                  pl.BlockSpec((None, None, nk, LANES, ATT_TK), lambda b, h, i: (b, h, 0, 0, 0)),
                  pl.BlockSpec((None, 2, 3, ATT_TK, ATT_TQ), lambda b, h, i: (h, 0, 0, 0, 0)),
                  vec, vec, vec, vec,
                  pl.BlockSpec((DA_V_DIM, 1), lambda b, h, i: (0, 0))],
        out_specs=pl.BlockSpec((None, ATT_TQ, LANES), lambda b, h, i: (b, i, h)),
        out_shape=jax.ShapeDtypeStruct((B, L, BRANCH_W), F32),
        scratch_shapes=[pltpu.VMEM((2, ATT_TQ, LANES), BF16),
                        pltpu.VMEM((2, 1, ATT_TQ), F32),
                        pltpu.VMEM((2, 1, ATT_TQ), F32),
                        pltpu.VMEM((2, DA_V_DIM, ATT_TQ), F32)],
        compiler_params=_cparams(("parallel", "parallel", "arbitrary"), 48),
        name="diff_attention",
    )(cfar, qk, qk, vt, bias, lq1, lk1, lq2, lk2, sg)


def _fft_consts(n2):
    n1 = FFT_N1
    n = n1 * n2
    f2 = np.exp(-2j * np.pi * np.outer(np.arange(n2), np.arange(n2)) / n2)
    tw = np.exp(-2j * np.pi * np.outer(np.arange(n2), np.arange(n1)) / n)
    f1 = np.exp(-2j * np.pi * np.outer(np.arange(n1), np.arange(n1)) / n1)
    c = {}
    c["f2_full"] = np.concatenate([f2.real, f2.imag], 0)
    c["f2_half"] = c["f2_full"][:, : n2 // 2]
    c["tre"], c["tim"] = tw.real, tw.imag
    c["g_top"] = np.concatenate([f1.real, f1.imag], 1)
    c["g_bot"] = np.concatenate([-f1.imag, f1.real], 1)
    c["gi_top"] = np.concatenate([f1.real, -f1.imag], 1)
    c["gi_bot"] = np.concatenate([f1.imag, f1.real], 1)
    c["fin_re"] = f2.real.T[: n2 // 2]
    c["fin_im"] = f2.imag.T[: n2 // 2]
    out = {}
    for k, v in c.items():
        out[k] = jnp.asarray(v, F32 if k in ("tre", "tim") else BF16)
    return out


def _filter_td_kernel(emb_ref, t_ref, w1_ref, b1_ref, f0_ref, w2_ref, b2_ref, f1_ref, w3_ref, ad_ref, o_ref):
    emb = emb_ref[...].astype(BF16)
    h = jnp.dot(w1_ref[...], emb, preferred_element_type=F32) + b1_ref[...]
    h = jnp.sin(f0_ref[...] * h)
    h = jnp.dot(w2_ref[...], h.astype(BF16), preferred_element_type=F32) + b2_ref[...]
    h = jnp.sin(f1_ref[...] * h)
    h = jnp.dot(w3_ref[...], h.astype(BF16), preferred_element_type=F32)
    decay = jnp.exp(-(ad_ref[...] * t_ref[...]))
    o_ref[...] = h * decay


def _filter_td(emb_t, t_full, w1t, b1, f0, w2t, b2, f1, w3t, absdelta):
    n = emb_t.shape[1]
    pt = 2048
    nch = n // pt
    col = lambda r: pl.BlockSpec((r, 1), lambda i: (0, 0))
    return pl.pallas_call(
        _filter_td_kernel,
        grid=(nch,),
        in_specs=[pl.BlockSpec((HY_EMB_PAD, pt), lambda i: (0, i)),
                  pl.BlockSpec((1, pt), lambda i: (0, i)),
                  pl.BlockSpec((HY_ORDER, HY_EMB_PAD), lambda i: (0, 0)), col(HY_ORDER), col(HY_ORDER),
                  pl.BlockSpec((HY_ORDER, HY_ORDER), lambda i: (0, 0)), col(HY_ORDER), col(HY_ORDER),
                  pl.BlockSpec((None, BRANCH_W, HY_ORDER), lambda i: (i // (nch // 2), 0, 0)),
                  col(BRANCH_W)],
        out_specs=pl.BlockSpec((BRANCH_W, pt), lambda i: (0, i)),
        out_shape=jax.ShapeDtypeStruct((BRANCH_W, n), F32),
        compiler_params=_cparams(("parallel",), 32),
        name="hyena_filter_time",
    )(emb_t, t_full, w1t, b1, f0, w2t, b2, f1, w3t, absdelta)


def _cmul(are, aim, bre, bim):
    return are * bre - aim * bim, are * bim + aim * bre


def _filter_fft_kernel(f_ref, f2_ref, tre_ref, tim_ref, gt_ref, gb_ref, o_ref, bre_ref, bim_ref, *, cc, n2):
    n_total = FFT_N1 * n2
    row = lax.broadcasted_iota(jnp.int32, (n2, FFT_N1), 0)
    lane = lax.broadcasted_iota(jnp.int32, (n2, FFT_N1), 1)
    is_lag0_backward = (row == n2 // 2) & (lane == 0)

    def per_channel(c, carry):
        f = f_ref[c]
        s = jnp.sum(jnp.abs(f), keepdims=True)
        f = jnp.where(is_lag0_backward, 0.0, f) * (1.0 / ((s + EPS) * n_total))
        a = jnp.dot(f2_ref[...], f.astype(BF16), preferred_element_type=F32)
        bre, bim = _cmul(a[:n2], a[n2:], tre_ref[...], tim_ref[...])
        bre_ref[c] = bre.astype(BF16)
        bim_ref[c] = bim.astype(BF16)
        return carry

    lax.fori_loop(0, cc, per_channel, 0)
    bre = bre_ref[...].reshape(cc * n2, FFT_N1)
    bim = bim_ref[...].reshape(cc * n2, FFT_N1)
    o_ref[...] = (jnp.dot(bre, gt_ref[...], preferred_element_type=F32)
                  + jnp.dot(bim, gb_ref[...], preferred_element_type=F32))


def _filter_fft(filt, consts, n2, cc):
    C = filt.shape[0]
    full = lambda a: pl.BlockSpec(a.shape, lambda i: (0,) * a.ndim)
    mats = [consts["f2_full"], consts["tre"], consts["tim"], consts["g_top"], consts["g_bot"]]
    return pl.pallas_call(
        functools.partial(_filter_fft_kernel, cc=cc, n2=n2),
        grid=(C // cc,),
        in_specs=[pl.BlockSpec((cc, n2, FFT_N1), lambda i: (i, 0, 0))] + [full(a) for a in mats],
        out_specs=pl.BlockSpec((cc * n2, 2 * FFT_N1), lambda i: (i, 0)),
        out_shape=jax.ShapeDtypeStruct((C * n2, 2 * FFT_N1), F32),
        scratch_shapes=[pltpu.VMEM((cc, n2, FFT_N1), BF16), pltpu.VMEM((cc, n2, FFT_N1), BF16)],
        compiler_params=_cparams(("parallel",), 40),
        name="hyena_filter_fft",
    )(filt, *mats)


def _shift_prev(x, row, lane):
    a = pltpu.roll(x, 1, axis=1)
    b = pltpu.roll(a, 1, axis=0)
    return jnp.where(lane == 0, jnp.where(row == 0, 0.0, b), a)


def _shift_next(x, row, lane, rows):
    a = pltpu.roll(x, FFT_N1 - 1, axis=1)
    b = pltpu.roll(a, rows - 1, axis=0)
    return jnp.where(lane == FFT_N1 - 1, jnp.where(row == rows - 1, 0.0, b), a)


def _hyena_kernel(sw_ref, sb_ref, d_ref, x0_ref, x1_ref, v_ref, hf_ref, f2_ref, tre_ref, tim_ref,
                  gt_ref, gb_ref, git_ref, gib_ref, fre_ref, fim_ref, o_ref,
                  z_ref, x0c_ref, bre_ref, bim_ref, c_ref, *, cc, n2):
    rows = n2 // 2
    ch0 = pl.program_id(1) * cc
    row = lax.broadcasted_iota(jnp.int32, (rows, FFT_N1), 0)
    lane = lax.broadcasted_iota(jnp.int32, (rows, FFT_N1), 1)

    def short_conv(x, ch):
        return (sw_ref[0, ch] * _shift_prev(x, row, lane) + sw_ref[1, ch] * x
                + sw_ref[2, ch] * _shift_next(x, row, lane, rows) + sb_ref[ch])

    def forward(c, carry):
        ch = ch0 + c
        x0c_ref[c] = short_conv(x0_ref[c], ch)
        z = short_conv(x1_ref[c], BRANCH_W + ch) * short_conv(v_ref[c], 2 * BRANCH_W + ch)
        z_ref[c] = z
        a = jnp.dot(f2_ref[...], z.astype(BF16), preferred_element_type=F32)
        bre, bim = _cmul(a[:n2], a[n2:], tre_ref[...], tim_ref[...])
        bre_ref[c] = bre.astype(BF16)
        bim_ref[c] = bim.astype(BF16)
        return carry

    lax.fori_loop(0, cc, forward, 0)

    bre = bre_ref[...].reshape(cc * n2, FFT_N1)
    bim = bim_ref[...].reshape(cc * n2, FFT_N1)
    x = (jnp.dot(bre, gt_ref[...], preferred_element_type=F32)
         + jnp.dot(bim, gb_ref[...], preferred_element_type=F32))
    hf = hf_ref[...]
    yre, yim = _cmul(x[:, :FFT_N1], x[:, FFT_N1:], hf[:, :FFT_N1], hf[:, FFT_N1:])
    cfull = (jnp.dot(yre.astype(BF16), git_ref[...], preferred_element_type=F32)
             + jnp.dot(yim.astype(BF16), gib_ref[...], preferred_element_type=F32))
    c_ref[...] = cfull.reshape(cc, n2, 2 * FFT_N1)

    def backward(c, carry):
        cm = c_ref[c]
        cre, cim = cm[:, :FFT_N1], cm[:, FFT_N1:]
        tre, tim = tre_ref[...], tim_ref[...]
        dre = cre * tre + cim * tim
        dim = cim * tre - cre * tim
        y = (jnp.dot(fre_ref[...], dre.astype(BF16), preferred_element_type=F32)
             + jnp.dot(fim_ref[...], dim.astype(BF16), preferred_element_type=F32))
        o_ref[c] = x0c_ref[c] * (y + z_ref[c] * d_ref[ch0 + c])
        return carry

    lax.fori_loop(0, cc, backward, 0)


def _hyena(u, hf, consts, short_w, short_b, d_bias, n2, cc):
    B = u.shape[0]
    rows = n2 // 2
    C = BRANCH_W
    nblk = C // cc
    smem = pl.BlockSpec(memory_space=pltpu.SMEM)
    full = lambda a: pl.BlockSpec(a.shape, lambda b, i: (0,) * a.ndim)
    mats = [consts[k] for k in ("f2_half", "tre", "tim", "g_top", "g_bot", "gi_top", "gi_bot", "fin_re", "fin_im")]
    sec = lambda s: pl.BlockSpec((None, cc, rows, FFT_N1), lambda b, i: (b, s * nblk + i, 0, 0))
    return pl.pallas_call(
        functools.partial(_hyena_kernel, cc=cc, n2=n2),
        grid=(B, nblk),
        in_specs=[smem, smem, smem, sec(0), sec(1), sec(2),
                  pl.BlockSpec((cc * n2, 2 * FFT_N1), lambda b, i: (i, 0))] + [full(a) for a in mats],
        out_specs=pl.BlockSpec((None, cc, rows, FFT_N1), lambda b, i: (b, i, 0, 0)),
        out_shape=jax.ShapeDtypeStruct((B, C, rows, FFT_N1), F32),
        scratch_shapes=[pltpu.VMEM((cc, rows, FFT_N1), F32), pltpu.VMEM((cc, rows, FFT_N1), F32),
                        pltpu.VMEM((cc, n2, FFT_N1), BF16), pltpu.VMEM((cc, n2, FFT_N1), BF16),
                        pltpu.VMEM((cc, n2, 2 * FFT_N1), F32)],
        compiler_params=_cparams(("parallel", "arbitrary"), 40),
        name="hyena_mixer",
    )(short_w, short_b, d_bias, u, u, u, hf, *mats)


def _hyena_tables(L):
    t = jnp.linspace(0.0, 1.0, L, dtype=F32)[:, None]
    w = 2.0 * math.pi * jnp.arange(L, dtype=F32)[:, None] / L
    bands = jnp.linspace(1e-4, HY_BANDS - 1, HY_BANDS, dtype=F32)[None, :]
    emb = jnp.concatenate([t, jnp.cos(bands * w), -jnp.sin(bands * w)], axis=-1)
    lag = np.concatenate([np.arange(L), [0], np.arange(L - 1, 0, -1)])
    emb_full = jnp.pad(emb[lag], ((0, 0), (0, HY_EMB_PAD - HY_EMB)))
    t_full = t[lag, 0][None, :]
    return emb_full.T, t_full


def _merge_kernel(x_ref, ya_ref, yb_ref, yct_ref, yd_ref, g_ref, ws_ref, wm_ref, wb_ref, wo_ref, fg_ref, o_ref,
                  *, final):
    x = x_ref[...]
    ms = jnp.mean(x * x, axis=-1, keepdims=True)
    h = (x * lax.rsqrt(ms + EPS) * g_ref[...]).astype(BF16)
    mixed = None
    for n in range(N_BRANCH):
        if n == 0:
            y = ya_ref[...]
        elif n == 1:
            y = yb_ref[...]
        elif n == 2:
            y = yct_ref[...].T
        else:
            y = yd_ref[...]
        sg = jnp.dot(h, ws_ref[n], preferred_element_type=F32)
        u = (y * (sg * jax.nn.sigmoid(sg))).astype(BF16)
        proj = jnp.dot(u, wb_ref[n], preferred_element_type=F32)
        merge = jax.nn.sigmoid(jnp.dot(h, wm_ref[n], preferred_element_type=F32))
        mixed = merge * proj if mixed is None else mixed + merge * proj
    out = x + jnp.dot(mixed.astype(BF16), wo_ref[...], preferred_element_type=F32)
    if final:
        ms = jnp.mean(out * out, axis=-1, keepdims=True)
        out = out * lax.rsqrt(ms + EPS) * fg_ref[...]
    o_ref[...] = out


def _merge(x2, ya, yb, yct, yd, g, ws, wm, wb, wo, fg, L, final):
    T, D = x2.shape
    tm = 256
    nl = L // tm
    W = BRANCH_W
    tok = lambda c: pl.BlockSpec((tm, c), lambda i: (i, 0))
    const = lambda a: pl.BlockSpec(a.shape, lambda i: (0,) * a.ndim, pipeline_mode=pl.Buffered(1))
    return pl.pallas_call(
        functools.partial(_merge_kernel, final=final),
        grid=(T // tm,),
        in_specs=[tok(D), tok(W), tok(W),
                  pl.BlockSpec((None, W, tm), lambda i: (i // nl, 0, i % nl)),
                  tok(W), const(g), const(ws), const(wm), const(wb), const(wo), const(fg)],
        out_specs=tok(D),
        out_shape=jax.ShapeDtypeStruct((T, D), F32),
        compiler_params=_cparams(("parallel",), 52),
        name="gated_merge",
    )(x2, ya, yb, yct, yd, g, ws, wm, wb, wo, fg)


def _layer_weights(l, norm_g, w_in, pool_w, pool_scale, subln_g, rel_bias, hy_short_w, hy_short_b, hy_w1, hy_b1,
                   hy_freq, hy_w2, hy_b2, hy_w3, hy_d, cf_dw_w, cf_dw_b, cf_ln_g, cf_ln_b, w_branch, w_out):
    w = w_in[l].astype(BF16)
    W = BRANCH_W
    p = {}
    p["g"] = norm_g[l][None, :]
    p["w_pool"] = w[:, OFF_POOL:OFF_Q]
    p["w_qk"] = w[:, OFF_Q:OFF_V]
    p["wt_v"] = w[:, OFF_V:OFF_HY].T
    p["wt_hy"] = w[:, OFF_HY:OFF_CF].T
    p["w_cf"] = w[:, OFF_CF:OFF_SILU]
    p["w_silu"] = w[:, OFF_SILU:OFF_MERGE].reshape(D_MODEL, N_BRANCH, W).transpose(1, 0, 2)
    p["w_merge"] = w[:, OFF_MERGE:].reshape(D_MODEL, N_BRANCH, D_MODEL).transpose(1, 0, 2)
    p["w_branch"] = w_branch[l].astype(BF16)
    p["w_out"] = w_out[l].astype(BF16)
    p["pool_w"] = pool_w[l].astype(BF16)
    p["pool_scale"] = pool_scale[l][None, :]
    p["subln_g"] = subln_g[l][:, None]
    buckets = _near_bias_buckets()
    table_t = rel_bias.astype(F32).T
    p["bias"] = table_t[:, buckets]
    p["cfar"] = jnp.stack([table_t[:, REL_BUCKETS // 2 - 1], table_t[:, REL_BUCKETS - 1]], axis=1)
    p["short_w"] = hy_short_w[l]
    p["short_b"] = hy_short_b[l]
    p["hy_d"] = hy_d[l]
    p["w1t"] = jnp.pad(hy_w1[l].T, ((0, 0), (0, HY_EMB_PAD - HY_EMB))).astype(BF16)
    p["b1"] = hy_b1[l][:, None]
    p["f0"] = hy_freq[l, 0][:, None]
    p["w2t"] = hy_w2[l].T.astype(BF16)
    p["b2"] = hy_b2[l][:, None]
    p["f1"] = hy_freq[l, 1][:, None]
    p["w3t"] = hy_w3[l].T.reshape(2, W, HY_ORDER).astype(BF16)
    max_decay = math.log(HY_DECAY_TARGET) / HY_FAST
    min_decay = math.log(HY_DECAY_TARGET) / HY_SLOW
    p["absdelta"] = jnp.abs(jnp.linspace(min_decay, max_decay, W, dtype=F32))[:, None]
    p["dw_w"] = cf_dw_w[l]
    p["dw_b"] = cf_dw_b[l][None, :]
    p["ln_g"] = cf_ln_g[l][None, :]
    p["ln_b"] = cf_ln_b[l][None, :]
    return p


def _hyena_chunk(n2):
    return max(1, 512 // n2)


def _filter_spectrum(p, L):
    n2 = 2 * L // FFT_N1
    consts = _fft_consts(n2)
    emb_t, t_full = _hyena_tables(L)
    filt = _filter_td(emb_t, t_full, p["w1t"], p["b1"], p["f0"], p["w2t"], p["b2"], p["f1"], p["w3t"],
                      p["absdelta"])
    hf = _filter_fft(filt.reshape(BRANCH_W, n2, FFT_N1), consts, n2, _hyena_chunk(n2))
    return hf, consts, n2


def _layer(x, p, l, lq1, lk1, lq2, lk2, final_g, final):
    B, L, D = x.shape
    T = B * L
    x2 = x.reshape(T, D)
    g = p["g"]
    lam_init = 0.8 - 0.6 * math.exp(-0.3 * l)

    pool_in = _proj(x2, g, p["w_pool"], F32, 512)
    qk = _proj(x2, g, p["w_qk"], BF16, 512).reshape(B, L, 2 * BRANCH_W)
    vt = _proj_t(x2, g, p["wt_v"], BF16, LANES, B, L, True)
    hyt = _proj_t(x2, g, p["wt_hy"], F32, 512, B, L, False)
    cf = _proj(x2, g, p["w_cf"], F32, 512)

    ya = _pool(pool_in, p["pool_w"], p["pool_scale"], L)
    yb = _attention(qk, vt, p["bias"], p["cfar"], lq1[l][None, :], lk1[l][None, :], lq2[l][None, :],
                    lk2[l][None, :], p["subln_g"], lam_init).reshape(T, BRANCH_W)
    hf, consts, n2 = _filter_spectrum(p, L)
    yct = _hyena(hyt.reshape(B, 3 * BRANCH_W, n2 // 2, FFT_N1), hf, consts, p["short_w"], p["short_b"], p["hy_d"],
                 n2, _hyena_chunk(n2)).reshape(B, BRANCH_W, L)
    yd = _conformer(cf, p["dw_w"], p["dw_b"], p["ln_g"], p["ln_b"], L)

    out = _merge(x2, ya, yb, yct, yd, g, p["w_silu"], p["w_merge"], p["w_branch"], p["w_out"], final_g[None, :], L,
                 final)
    return out.reshape(B, L, D)


def kernel(x_prompt, x_sample, norm_g, w_in, pool_w, pool_scale, lambda_q1, lambda_k1, lambda_q2, lambda_k2, subln_g, rel_bias, hy_short_w, hy_short_b, hy_w1, hy_b1, hy_freq, hy_w2, hy_b2, hy_w3, hy_d, cf_dw_w, cf_dw_b, cf_ln_g, cf_ln_b, w_branch, w_out, final_g):
    depth = norm_g.shape[0]
    hp, hs = x_prompt, x_sample
    for l in range(depth):
        p = _layer_weights(l, norm_g, w_in, pool_w, pool_scale, subln_g, rel_bias, hy_short_w, hy_short_b, hy_w1,
                           hy_b1, hy_freq, hy_w2, hy_b2, hy_w3, hy_d, cf_dw_w, cf_dw_b, cf_ln_g, cf_ln_b, w_branch,
                           w_out)
        final = l == depth - 1
        hp = _layer(hp, p, l, lambda_q1, lambda_k1, lambda_q2, lambda_k2, final_g, final)
        hs = _layer(hs, p, l, lambda_q1, lambda_k1, lambda_q2, lambda_k2, final_g, final)
    return (hp, hs)
```

```python
import functools
import math

import numpy as np
import jax
import jax.numpy as jnp
from jax import lax
from jax.experimental import pallas as pl
from jax.experimental.pallas import tpu as pltpu

F32 = jnp.float32
BF16 = jnp.bfloat16

D_MODEL = 1024
BRANCH_W = 512
N_BRANCH = 4
POOL_WINDOWS = (2, 4, 8, 16)
POOL_GW = 128
DA_HEADS = 4
DA_HEAD_DIM = 64
DA_V_DIM = 128
REL_BUCKETS = 32
REL_MAX_DIST = 128
HY_EMB = 33
HY_EMB_PAD = 40
HY_BANDS = 16
HY_ORDER = 64
HY_DECAY_TARGET = 1e-2
HY_FAST = 0.3
HY_SLOW = 1.5
CF_WIDTH = 31
EPS = 1e-6

OFF_POOL = 0
OFF_Q = 512
OFF_K = 1024
OFF_V = 1536
OFF_HY = 2048
OFF_CF = 3584
OFF_SILU = 4608
OFF_MERGE = 6656

LANES = 128
FFT_N1 = 256
HALO = 16
TOK_TILE = 512
ATT_TQ = 256
ATT_TK = 512
NEG_BIG = -1e30
V_ROWS = DA_V_DIM + 16


def _cparams(sem, vmem_mb):
    return pltpu.CompilerParams(dimension_semantics=sem, vmem_limit_bytes=vmem_mb << 20)


def _proj_kernel(x_ref, g_ref, w_ref, o_ref, h_ref, *, transposed, ones_rows=0):
    @pl.when(pl.program_id(1) == 0)
    def _():
        x = x_ref[...]
        ms = jnp.mean(x * x, axis=-1, keepdims=True)
        h_ref[...] = (x * lax.rsqrt(ms + EPS) * g_ref[...]).astype(BF16)

    h = h_ref[...]
    if transposed:
        o = lax.dot_general(w_ref[...], h, (((1,), (1,)), ((), ())), preferred_element_type=F32)
    else:
        o = jnp.dot(h, w_ref[...], preferred_element_type=F32)
    if ones_rows:
        n = o.shape[0]
        o_ref[pl.ds(0, n), :] = o.astype(o_ref.dtype)
        o_ref[pl.ds(n, ones_rows), :] = jnp.ones((ones_rows, o.shape[1]), o_ref.dtype)
    else:
        o_ref[...] = o.astype(o_ref.dtype)


def _proj(x2, g, w, out_dtype, tn):
    T, D = x2.shape
    N = w.shape[1]
    tm = TOK_TILE
    return pl.pallas_call(
        functools.partial(_proj_kernel, transposed=False),
        grid=(T // tm, N // tn),
        in_specs=[pl.BlockSpec((tm, D), lambda i, j: (i, 0)),
                  pl.BlockSpec((1, D), lambda i, j: (0, 0)),
                  pl.BlockSpec((D, tn), lambda i, j: (0, j))],
        out_specs=pl.BlockSpec((tm, tn), lambda i, j: (i, j)),
        out_shape=jax.ShapeDtypeStruct((T, N), out_dtype),
        scratch_shapes=[pltpu.VMEM((tm, D), BF16)],
        compiler_params=_cparams(("parallel", "arbitrary"), 40),
        name="proj_tok",
    )(x2, g, w)


def _proj_t(x2, g, wt, out_dtype, tn, B, L, head_layout):
    T, D = x2.shape
    N = wt.shape[0]
    tm = TOK_TILE
    nl = L // tm
    if head_layout:
        assert tn == LANES
        out_shape = jax.ShapeDtypeStruct((B, N // tn, nl, V_ROWS, tm), out_dtype)
        out_spec = pl.BlockSpec((None, None, None, V_ROWS, tm), lambda i, j: (i // nl, j, i % nl, 0, 0))
    else:
        out_shape = jax.ShapeDtypeStruct((B, N, L), out_dtype)
        out_spec = pl.BlockSpec((None, tn, tm), lambda i, j: (i // nl, j, i % nl))
    return pl.pallas_call(
        functools.partial(_proj_kernel, transposed=True, ones_rows=V_ROWS - tn if head_layout else 0),
        grid=(T // tm, N // tn),
        in_specs=[pl.BlockSpec((tm, D), lambda i, j: (i, 0)),
                  pl.BlockSpec((1, D), lambda i, j: (0, 0)),
                  pl.BlockSpec((tn, D), lambda i, j: (j, 0))],
        out_specs=out_spec,
        out_shape=out_shape,
        scratch_shapes=[pltpu.VMEM((tm, D), BF16)],
        compiler_params=_cparams(("parallel", "arbitrary"), 40),
        name="proj_chan",
    )(x2, g, wt)


def _halo_specs(tm, C, T):
    r = tm // HALO
    last = T // HALO - 1
    return [pl.BlockSpec((tm, C), lambda i: (i, 0)),
            pl.BlockSpec((HALO, C), lambda i: (jnp.maximum(i * r - 1, 0), 0)),
            pl.BlockSpec((HALO, C), lambda i: (jnp.minimum((i + 1) * r, last), 0))]


def _fill_ext(ext_ref, cur_ref, prev_ref, next_ref, tm, L):
    i = pl.program_id(0)
    tiles_per_seq = L // tm
    first = (i % tiles_per_seq) == 0
    lastt = (i % tiles_per_seq) == tiles_per_seq - 1
    ext_ref[pl.ds(0, HALO), :] = jnp.where(first, 0.0, prev_ref[...])
    ext_ref[pl.ds(HALO, tm), :] = cur_ref[...]
    ext_ref[pl.ds(HALO + tm, HALO), :] = jnp.where(lastt, 0.0, next_ref[...])


def _pool_kernel(cur_ref, prev_ref, next_ref, w_ref, sc_ref, o_ref, ext_ref, *, tm, L):
    _fill_ext(ext_ref, cur_ref, prev_ref, next_ref, tm, L)
    i = pl.program_id(0)
    pos = (i % (L // tm)) * tm + lax.broadcasted_iota(jnp.int32, (tm, 1), 0)
    for g, w in enumerate(POOL_WINDOWS):
        lanes = pl.ds(g * POOL_GW, POOL_GW)
        s = None
        for d in range(-(w // 2), w - w // 2):
            t = ext_ref[pl.ds(HALO + d, tm), lanes]
            s = t if s is None else s + t
        lo = jnp.maximum(pos - w // 2, 0)
        hi = jnp.minimum(pos + (w - 1 - w // 2), L - 1)
        cnt = (hi - lo + 1).astype(F32)
        pooled = s / cnt - ext_ref[pl.ds(HALO, tm), lanes]
        mixed = jnp.dot(pooled.astype(BF16), w_ref[g], preferred_element_type=F32)
        o_ref[:, lanes] = mixed * sc_ref[:, lanes]


def _pool(u, w_mix, scale, L):
    T, C = u.shape
    tm = TOK_TILE
    return pl.pallas_call(
        functools.partial(_pool_kernel, tm=tm, L=L),
        grid=(T // tm,),
        in_specs=_halo_specs(tm, C, T) + [
            pl.BlockSpec((4, POOL_GW, POOL_GW), lambda i: (0, 0, 0)),
            pl.BlockSpec((1, C), lambda i: (0, 0))],
        out_specs=pl.BlockSpec((tm, C), lambda i: (i, 0)),
        out_shape=jax.ShapeDtypeStruct((T, C), F32),
        scratch_shapes=[pltpu.VMEM((tm + 2 * HALO, C), F32)],
        compiler_params=_cparams(("parallel",), 32),
        name="pool_mixer",
    )(u, u, u, w_mix, scale)


def _conf_kernel(cur_ref, prev_ref, next_ref, w_ref, b_ref, g_ref, beta_ref, o_ref, ext_ref, h_ref, *, tm, L):
    _fill_ext(ext_ref, cur_ref, prev_ref, next_ref, tm, L)
    a = ext_ref[:, pl.ds(0, BRANCH_W)]
    gt = ext_ref[:, pl.ds(BRANCH_W, BRANCH_W)]
    h_ref[...] = a * jax.nn.sigmoid(gt)
    acc = jnp.zeros((tm, BRANCH_W), F32) + b_ref[...]
    for j in range(CF_WIDTH):
        acc = acc + w_ref[pl.ds(j, 1), :] * h_ref[pl.ds(HALO - CF_WIDTH // 2 + j, tm), :]
    mu = jnp.mean(acc, axis=-1, keepdims=True)
    cen = acc - mu
    var = jnp.mean(cen * cen, axis=-1, keepdims=True)
    y = cen * lax.rsqrt(var + EPS) * g_ref[...] + beta_ref[...]
    o_ref[...] = y * jax.nn.sigmoid(y)


def _conformer(u, dw_w, dw_b, ln_g, ln_b, L):
    T, C2 = u.shape
    C = C2 // 2
    tm = TOK_TILE
    vec = pl.BlockSpec((1, C), lambda i: (0, 0))
    return pl.pallas_call(
        functools.partial(_conf_kernel, tm=tm, L=L),
        grid=(T // tm,),
        in_specs=_halo_specs(tm, C2, T) + [pl.BlockSpec((CF_WIDTH, C), lambda i: (0, 0)), vec, vec, vec],
        out_specs=pl.BlockSpec((tm, C), lambda i: (i, 0)),
        out_shape=jax.ShapeDtypeStruct((T, C), F32),
        scratch_shapes=[pltpu.VMEM((tm + 2 * HALO, C2), F32), pltpu.VMEM((tm + 2 * HALO, C), F32)],
        compiler_params=_cparams(("parallel",), 32),
        name="conformer_conv",
    )(u, u, u, dw_w, dw_b, ln_g, ln_b)


def _rel_bucket_np(rel):
    nb = REL_BUCKETS // 2
    max_exact = nb // 2
    ret = np.where(rel > 0, nb, 0)
    n = np.abs(rel)
    ratio = np.log(np.maximum(n, 1).astype(np.float32) / np.float32(max_exact)) / np.float32(
        math.log(REL_MAX_DIST / max_exact))
    large = max_exact + (ratio * np.float32(nb - max_exact)).astype(np.int32)
    large = np.minimum(large, nb - 1)
    return ret + np.where(n < max_exact, n, large)


def _bias_buckets():
    r = np.arange(ATT_TK)[:, None]
    c = np.arange(ATT_TQ)[None, :]
    out = np.zeros((2, 5, ATT_TK, ATT_TQ), np.int32)
    for e in range(2):
        for dj in range(-2, 3):
            out[e, dj + 2] = _rel_bucket_np(ATT_TK * dj + r - ATT_TQ * e - c)
    assert (out[:, 0] == REL_BUCKETS // 2 - 1).all() and (out[:, 4] == REL_BUCKETS - 1).all()
    return out


def _attn_kernel(q_ref, k_ref, vt_ref, bias_ref, lq1_ref, lk1_ref, lq2_ref, lk2_ref, sg_ref,
                 o_ref, qm_ref, s_ref, p_ref, al_ref, m_ref, acc_ref, *, nk, lam_init):
    qi = pl.program_id(2)
    q = q_ref[...]
    lane = lax.broadcasted_iota(jnp.int32, q.shape, 1)
    qs = q * jnp.asarray(DA_HEAD_DIM ** -0.5, BF16)
    zero = jnp.zeros_like(qs)
    qm_ref[0] = jnp.where(lane < DA_HEAD_DIM, qs, zero)
    qm_ref[1] = jnp.where(lane >= DA_HEAD_DIM, qs, zero)
    m_ref[...] = jnp.full(m_ref.shape, NEG_BIG, F32)
    acc_ref[...] = jnp.zeros(acc_ref.shape, F32)
    p_ref[1] = jnp.zeros(p_ref.shape[1:], BF16)
    al_ref[1] = jnp.ones(al_ref.shape[1:], F32)
    j0 = qi // 2
    e = qi % 2

    def scores(j, slot):
        kblk = k_ref[pl.ds(pl.multiple_of(j * ATT_TK, ATT_TK), ATT_TK), :]
        bias = bias_ref[e, jnp.clip(j - j0, -2, 2) + 2]
        for mm in range(2):
            s = lax.dot_general(kblk, qm_ref[mm], (((1,), (1,)), ((), ())), preferred_element_type=F32)
            s_ref[slot, mm] = s + bias

    def softmax(slot):
        for mm in range(2):
            s = s_ref[slot, mm]
            m_prev = m_ref[mm]
            m_new = jnp.maximum(m_prev, jnp.max(s, axis=0, keepdims=True))
            al_ref[slot, mm] = jnp.exp(m_prev - m_new)
            p_ref[slot, mm] = jnp.exp(s - m_new).astype(BF16)
            m_ref[mm] = m_new

    def weighted_values(j, slot):
        vblk = vt_ref[j]
        for mm in range(2):
            acc_ref[mm] = al_ref[slot, mm] * acc_ref[mm] + jnp.dot(vblk, p_ref[slot, mm],
                                                                   preferred_element_type=F32)

    scores(0, 0)

    def body(i, carry):
        j = 2 * i
        scores(j + 1, 1)
        softmax(0)
        weighted_values(jnp.maximum(j - 1, 0), 1)
        scores(jnp.minimum(j + 2, nk - 1), 0)
        softmax(1)
        weighted_values(j, 0)
        return carry

    lax.fori_loop(0, nk // 2, body, 0)
    weighted_values(nk - 1, 1)

    lam = (jnp.exp(jnp.sum(lq1_ref[...] * lk1_ref[...], keepdims=True))
           - jnp.exp(jnp.sum(lq2_ref[...] * lk2_ref[...], keepdims=True)) + lam_init)
    a0 = acc_ref[0]
    a1 = acc_ref[1]
    o = (a0[:DA_V_DIM] / a0[DA_V_DIM:DA_V_DIM + 1]
         - lam * (a1[:DA_V_DIM] / a1[DA_V_DIM:DA_V_DIM + 1]))
    ms = jnp.mean(o * o, axis=0, keepdims=True)
    y = o * lax.rsqrt(ms + EPS) * sg_ref[...] * (1.0 - lam_init)
    o_ref[...] = y.T


def _attention(qk, vt, bias, lq1, lk1, lq2, lk2, sg, lam_init):
    B, L, _ = qk.shape
    nk = L // ATT_TK
    assert nk % 2 == 0
    vec = pl.BlockSpec((1, DA_HEAD_DIM), lambda b, h, i: (0, 0))
    return pl.pallas_call(
        functools.partial(_attn_kernel, nk=nk, lam_init=lam_init),
        grid=(B, DA_HEADS, L // ATT_TQ),
        in_specs=[pl.BlockSpec((None, ATT_TQ, LANES), lambda b, h, i: (b, i, h)),
                  pl.BlockSpec((None, L, LANES), lambda b, h, i: (b, 0, DA_HEADS + h)),
                  pl.BlockSpec((None, None, nk, V_ROWS, ATT_TK), lambda b, h, i: (b, h, 0, 0, 0)),
                  pl.BlockSpec((None, 2, 5, ATT_TK, ATT_TQ), lambda b, h, i: (h, 0, 0, 0, 0)),
                  vec, vec, vec, vec,
                  pl.BlockSpec((DA_V_DIM, 1), lambda b, h, i: (0, 0))],
        out_specs=pl.BlockSpec((None, ATT_TQ, LANES), lambda b, h, i: (b, i, h)),
        out_shape=jax.ShapeDtypeStruct((B, L, BRANCH_W), F32),
        scratch_shapes=[pltpu.VMEM((2, ATT_TQ, LANES), BF16),
                        pltpu.VMEM((2, 2, ATT_TK, ATT_TQ), F32),
                        pltpu.VMEM((2, 2, ATT_TK, ATT_TQ), BF16),
                        pltpu.VMEM((2, 2, 1, ATT_TQ), F32),
                        pltpu.VMEM((2, 1, ATT_TQ), F32),
                        pltpu.VMEM((2, V_ROWS, ATT_TQ), F32)],
        compiler_params=_cparams(("parallel", "parallel", "arbitrary"), 52),
        name="diff_attention",
    )(qk, qk, vt, bias, lq1, lk1, lq2, lk2, sg)


def _fft_consts(n2):
    n1 = FFT_N1
    n = n1 * n2
    f2 = np.exp(-2j * np.pi * np.outer(np.arange(n2), np.arange(n2)) / n2)
    tw = np.exp(-2j * np.pi * np.outer(np.arange(n2), np.arange(n1)) / n)
    f1 = np.exp(-2j * np.pi * np.outer(np.arange(n1), np.arange(n1)) / n1)
    c = {}
    c["f2_full"] = np.concatenate([f2.real, f2.imag], 0)
    c["f2_half"] = c["f2_full"][:, : n2 // 2]
    c["tre"], c["tim"] = tw.real, tw.imag
    c["g_top"] = np.concatenate([f1.real, f1.imag], 1)
    c["g_bot"] = np.concatenate([-f1.imag, f1.real], 1)
    c["gi_top"] = np.concatenate([f1.real, -f1.imag], 1)
    c["gi_bot"] = np.concatenate([f1.imag, f1.real], 1)
    c["fin_re"] = f2.real.T[: n2 // 2]
    c["fin_im"] = f2.imag.T[: n2 // 2]
    out = {}
    for k, v in c.items():
        out[k] = jnp.asarray(v, F32 if k in ("tre", "tim") else BF16)
    return out


def _filter_td_kernel(emb_ref, t_ref, w1_ref, b1_ref, f0_ref, w2_ref, b2_ref, f1_ref, w3_ref, ad_ref, o_ref):
    emb = emb_ref[...].astype(BF16)
    h = jnp.dot(w1_ref[...], emb, preferred_element_type=F32) + b1_ref[...]
    h = jnp.sin(f0_ref[...] * h)
    h = jnp.dot(w2_ref[...], h.astype(BF16), preferred_element_type=F32) + b2_ref[...]
    h = jnp.sin(f1_ref[...] * h)
    h = jnp.dot(w3_ref[...], h.astype(BF16), preferred_element_type=F32)
    decay = jnp.exp(-(ad_ref[...] * t_ref[...]))
    o_ref[...] = h * decay


def _filter_td(emb_t, t_full, w1t, b1, f0, w2t, b2, f1, w3t, absdelta):
    n = emb_t.shape[1]
    pt = 2048
    nch = n // pt
    col = lambda r: pl.BlockSpec((r, 1), lambda i: (0, 0))
    return pl.pallas_call(
        _filter_td_kernel,
        grid=(nch,),
        in_specs=[pl.BlockSpec((HY_EMB_PAD, pt), lambda i: (0, i)),
                  pl.BlockSpec((1, pt), lambda i: (0, i)),
                  pl.BlockSpec((HY_ORDER, HY_EMB_PAD), lambda i: (0, 0)), col(HY_ORDER), col(HY_ORDER),
                  pl.BlockSpec((HY_ORDER, HY_ORDER), lambda i: (0, 0)), col(HY_ORDER), col(HY_ORDER),
                  pl.BlockSpec((None, BRANCH_W, HY_ORDER), lambda i: (i // (nch // 2), 0, 0)),
                  col(BRANCH_W)],
        out_specs=pl.BlockSpec((BRANCH_W, pt), lambda i: (0, i)),
        out_shape=jax.ShapeDtypeStruct((BRANCH_W, n), F32),
        compiler_params=_cparams(("parallel",), 32),
        name="hyena_filter_time",
    )(emb_t, t_full, w1t, b1, f0, w2t, b2, f1, w3t, absdelta)


def _cmul(are, aim, bre, bim):
    return are * bre - aim * bim, are * bim + aim * bre


def _filter_fft_kernel(f_ref, f2_ref, tre_ref, tim_ref, gt_ref, gb_ref, o_ref, bre_ref, bim_ref, *, cc, n2):
    n_total = FFT_N1 * n2
    row = lax.broadcasted_iota(jnp.int32, (n2, FFT_N1), 0)
    lane = lax.broadcasted_iota(jnp.int32, (n2, FFT_N1), 1)
    is_lag0_backward = (row == n2 // 2) & (lane == 0)

    def per_channel(c, carry):
        f = f_ref[c]
        s = jnp.sum(jnp.abs(f), keepdims=True)
        f = jnp.where(is_lag0_backward, 0.0, f) * (1.0 / ((s + EPS) * n_total))
        a = jnp.dot(f2_ref[...], f.astype(BF16), preferred_element_type=F32)
        bre, bim = _cmul(a[:n2], a[n2:], tre_ref[...], tim_ref[...])
        bre_ref[c] = bre.astype(BF16)
        bim_ref[c] = bim.astype(BF16)
        return carry

    lax.fori_loop(0, cc, per_channel, 0)
    bre = bre_ref[...].reshape(cc * n2, FFT_N1)
    bim = bim_ref[...].reshape(cc * n2, FFT_N1)
    o_ref[...] = (jnp.dot(bre, gt_ref[...], preferred_element_type=F32)
                  + jnp.dot(bim, gb_ref[...], preferred_element_type=F32))


def _filter_fft(filt, consts, n2, cc):
    C = filt.shape[0]
    full = lambda a: pl.BlockSpec(a.shape, lambda i: (0,) * a.ndim)
    mats = [consts["f2_full"], consts["tre"], consts["tim"], consts["g_top"], consts["g_bot"]]
    return pl.pallas_call(
        functools.partial(_filter_fft_kernel, cc=cc, n2=n2),
        grid=(C // cc,),
        in_specs=[pl.BlockSpec((cc, n2, FFT_N1), lambda i: (i, 0, 0))] + [full(a) for a in mats],
        out_specs=pl.BlockSpec((cc * n2, 2 * FFT_N1), lambda i: (i, 0)),
        out_shape=jax.ShapeDtypeStruct((C * n2, 2 * FFT_N1), F32),
        scratch_shapes=[pltpu.VMEM((cc, n2, FFT_N1), BF16), pltpu.VMEM((cc, n2, FFT_N1), BF16)],
        compiler_params=_cparams(("parallel",), 40),
        name="hyena_filter_fft",
    )(filt, *mats)


def _shift_prev(x, row, lane):
    a = pltpu.roll(x, 1, axis=1)
    b = pltpu.roll(a, 1, axis=0)
    return jnp.where(lane == 0, jnp.where(row == 0, 0.0, b), a)


def _shift_next(x, row, lane, rows):
    a = pltpu.roll(x, FFT_N1 - 1, axis=1)
    b = pltpu.roll(a, rows - 1, axis=0)
    return jnp.where(lane == FFT_N1 - 1, jnp.where(row == rows - 1, 0.0, b), a)


def _hyena_kernel(sw_ref, sb_ref, d_ref, x0_ref, x1_ref, v_ref, hf_ref, f2_ref, tre_ref, tim_ref,
                  gt_ref, gb_ref, git_ref, gib_ref, fre_ref, fim_ref, o_ref,
                  z_ref, x0c_ref, bre_ref, bim_ref, c_ref, *, cc, n2):
    rows = n2 // 2
    ch0 = pl.program_id(1) * cc
    row = lax.broadcasted_iota(jnp.int32, (rows, FFT_N1), 0)
    lane = lax.broadcasted_iota(jnp.int32, (rows, FFT_N1), 1)

    def short_conv(x, ch):
        return (sw_ref[0, ch] * _shift_prev(x, row, lane) + sw_ref[1, ch] * x
                + sw_ref[2, ch] * _shift_next(x, row, lane, rows) + sb_ref[ch])

    def forward(c, carry):
        ch = ch0 + c
        x0c_ref[c] = short_conv(x0_ref[c], ch)
        z = short_conv(x1_ref[c], BRANCH_W + ch) * short_conv(v_ref[c], 2 * BRANCH_W + ch)
        z_ref[c] = z
        a = jnp.dot(f2_ref[...], z.astype(BF16), preferred_element_type=F32)
        bre, bim = _cmul(a[:n2], a[n2:], tre_ref[...], tim_ref[...])
        bre_ref[c] = bre.astype(BF16)
        bim_ref[c] = bim.astype(BF16)
        return carry

    lax.fori_loop(0, cc, forward, 0)

    bre = bre_ref[...].reshape(cc * n2, FFT_N1)
    bim = bim_ref[...].reshape(cc * n2, FFT_N1)
    x = (jnp.dot(bre, gt_ref[...], preferred_element_type=F32)
         + jnp.dot(bim, gb_ref[...], preferred_element_type=F32))
    hf = hf_ref[...]
    yre, yim = _cmul(x[:, :FFT_N1], x[:, FFT_N1:], hf[:, :FFT_N1], hf[:, FFT_N1:])
    cfull = (jnp.dot(yre.astype(BF16), git_ref[...], preferred_element_type=F32)
             + jnp.dot(yim.astype(BF16), gib_ref[...], preferred_element_type=F32))
    c_ref[...] = cfull.reshape(cc, n2, 2 * FFT_N1)

    def backward(c, carry):
        cm = c_ref[c]
        cre, cim = cm[:, :FFT_N1], cm[:, FFT_N1:]
        tre, tim = tre_ref[...], tim_ref[...]
        dre = cre * tre + cim * tim
        dim = cim * tre - cre * tim
        y = (jnp.dot(fre_ref[...], dre.astype(BF16), preferred_element_type=F32)
             + jnp.dot(fim_ref[...], dim.astype(BF16), preferred_element_type=F32))
        o_ref[c] = x0c_ref[c] * (y + z_ref[c] * d_ref[ch0 + c])
        return carry

    lax.fori_loop(0, cc, backward, 0)


def _hyena(u, hf, consts, short_w, short_b, d_bias, n2, cc):
    B = u.shape[0]
    rows = n2 // 2
    C = BRANCH_W
    nblk = C // cc
    smem = pl.BlockSpec(memory_space=pltpu.SMEM)
    full = lambda a: pl.BlockSpec(a.shape, lambda b, i: (0,) * a.ndim)
    mats = [consts[k] for k in ("f2_half", "tre", "tim", "g_top", "g_bot", "gi_top", "gi_bot", "fin_re", "fin_im")]
    sec = lambda s: pl.BlockSpec((None, cc, rows, FFT_N1), lambda b, i: (b, s * nblk + i, 0, 0))
    return pl.pallas_call(
        functools.partial(_hyena_kernel, cc=cc, n2=n2),
        grid=(B, nblk),
        in_specs=[smem, smem, smem, sec(0), sec(1), sec(2),
                  pl.BlockSpec((cc * n2, 2 * FFT_N1), lambda b, i: (i, 0))] + [full(a) for a in mats],
        out_specs=pl.BlockSpec((None, cc, rows, FFT_N1), lambda b, i: (b, i, 0, 0)),
        out_shape=jax.ShapeDtypeStruct((B, C, rows, FFT_N1), F32),
        scratch_shapes=[pltpu.VMEM((cc, rows, FFT_N1), F32), pltpu.VMEM((cc, rows, FFT_N1), F32),
                        pltpu.VMEM((cc, n2, FFT_N1), BF16), pltpu.VMEM((cc, n2, FFT_N1), BF16),
                        pltpu.VMEM((cc, n2, 2 * FFT_N1), F32)],
        compiler_params=_cparams(("parallel", "arbitrary"), 40),
        name="hyena_mixer",
    )(short_w, short_b, d_bias, u, u, u, hf, *mats)


def _hyena_tables(L):
    t = jnp.linspace(0.0, 1.0, L, dtype=F32)[:, None]
    w = 2.0 * math.pi * jnp.arange(L, dtype=F32)[:, None] / L
    bands = jnp.linspace(1e-4, HY_BANDS - 1, HY_BANDS, dtype=F32)[None, :]
    emb = jnp.concatenate([t, jnp.cos(bands * w), -jnp.sin(bands * w)], axis=-1)
    two_sided = lambda a: jnp.concatenate([a, a[:1], a[:0:-1]], axis=0)
    emb_full = jnp.pad(two_sided(emb), ((0, 0), (0, HY_EMB_PAD - HY_EMB)))
    t_full = two_sided(t).T
    return emb_full.T, t_full


def _merge_kernel(x_ref, ya_ref, yb_ref, yct_ref, yd_ref, g_ref, ws_ref, wm_ref, wb_ref, wo_ref, fg_ref, o_ref,
                  *, final):
    x = x_ref[...]
    ms = jnp.mean(x * x, axis=-1, keepdims=True)
    h = (x * lax.rsqrt(ms + EPS) * g_ref[...]).astype(BF16)
    mixed = None
    for n in range(N_BRANCH):
        if n == 0:
            y = ya_ref[...]
        elif n == 1:
            y = yb_ref[...]
        elif n == 2:
            y = yct_ref[...].T
        else:
            y = yd_ref[...]
        sg = jnp.dot(h, ws_ref[n], preferred_element_type=F32)
        u = (y * (sg * jax.nn.sigmoid(sg))).astype(BF16)
        proj = jnp.dot(u, wb_ref[n], preferred_element_type=F32)
        merge = jax.nn.sigmoid(jnp.dot(h, wm_ref[n], preferred_element_type=F32))
        mixed = merge * proj if mixed is None else mixed + merge * proj
    out = x + jnp.dot(mixed.astype(BF16), wo_ref[...], preferred_element_type=F32)
    if final:
        ms = jnp.mean(out * out, axis=-1, keepdims=True)
        out = out * lax.rsqrt(ms + EPS) * fg_ref[...]
    o_ref[...] = out


def _merge(x2, ya, yb, yct, yd, g, ws, wm, wb, wo, fg, L, final):
    T, D = x2.shape
    tm = 256
    nl = L // tm
    W = BRANCH_W
    tok = lambda c: pl.BlockSpec((tm, c), lambda i: (i, 0))
    const = lambda a: pl.BlockSpec(a.shape, lambda i: (0,) * a.ndim, pipeline_mode=pl.Buffered(1))
    return pl.pallas_call(
        functools.partial(_merge_kernel, final=final),
        grid=(T // tm,),
        in_specs=[tok(D), tok(W), tok(W),
                  pl.BlockSpec((None, W, tm), lambda i: (i // nl, 0, i % nl)),
                  tok(W), const(g), const(ws), const(wm), const(wb), const(wo), const(fg)],
        out_specs=tok(D),
        out_shape=jax.ShapeDtypeStruct((T, D), F32),
        compiler_params=_cparams(("parallel",), 52),
        name="gated_merge",
    )(x2, ya, yb, yct, yd, g, ws, wm, wb, wo, fg)


def _layer_weights(l, norm_g, w_in, pool_w, pool_scale, subln_g, rel_bias, hy_short_w, hy_short_b, hy_w1, hy_b1,
                   hy_freq, hy_w2, hy_b2, hy_w3, hy_d, cf_dw_w, cf_dw_b, cf_ln_g, cf_ln_b, w_branch, w_out):
    w = w_in[l].astype(BF16)
    W = BRANCH_W
    p = {}
    p["g"] = norm_g[l][None, :]
    p["w_pool"] = w[:, OFF_POOL:OFF_Q]
    p["w_qk"] = w[:, OFF_Q:OFF_V]
    p["wt_v"] = w[:, OFF_V:OFF_HY].T
    p["wt_hy"] = w[:, OFF_HY:OFF_CF].T
    p["w_cf"] = w[:, OFF_CF:OFF_SILU]
    p["w_silu"] = w[:, OFF_SILU:OFF_MERGE].reshape(D_MODEL, N_BRANCH, W).transpose(1, 0, 2)
    p["w_merge"] = w[:, OFF_MERGE:].reshape(D_MODEL, N_BRANCH, D_MODEL).transpose(1, 0, 2)
    p["w_branch"] = w_branch[l].astype(BF16)
    p["w_out"] = w_out[l].astype(BF16)
    p["pool_w"] = pool_w[l].astype(BF16)
    p["pool_scale"] = pool_scale[l][None, :]
    p["subln_g"] = subln_g[l][:, None]
    buckets = jnp.asarray(_bias_buckets())[None]
    table_t = rel_bias.astype(F32).T
    bias = jnp.zeros((DA_HEADS,) + buckets.shape[1:], F32)
    for b in range(REL_BUCKETS):
        bias = jnp.where(buckets == b, table_t[:, b][:, None, None, None, None], bias)
    p["bias"] = bias
    p["short_w"] = hy_short_w[l]
    p["short_b"] = hy_short_b[l]
    p["hy_d"] = hy_d[l]
    p["w1t"] = jnp.pad(hy_w1[l].T, ((0, 0), (0, HY_EMB_PAD - HY_EMB))).astype(BF16)
    p["b1"] = hy_b1[l][:, None]
    p["f0"] = hy_freq[l, 0][:, None]
    p["w2t"] = hy_w2[l].T.astype(BF16)
    p["b2"] = hy_b2[l][:, None]
    p["f1"] = hy_freq[l, 1][:, None]
    p["w3t"] = hy_w3[l].T.reshape(2, W, HY_ORDER).astype(BF16)
    max_decay = math.log(HY_DECAY_TARGET) / HY_FAST
    min_decay = math.log(HY_DECAY_TARGET) / HY_SLOW
    p["absdelta"] = jnp.abs(jnp.linspace(min_decay, max_decay, W, dtype=F32))[:, None]
    p["dw_w"] = cf_dw_w[l]
    p["dw_b"] = cf_dw_b[l][None, :]
    p["ln_g"] = cf_ln_g[l][None, :]
    p["ln_b"] = cf_ln_b[l][None, :]
    return p


def _hyena_chunk(n2):
    return max(1, 512 // n2)


def _filter_spectrum(p, L):
    n2 = 2 * L // FFT_N1
    consts = _fft_consts(n2)
    emb_t, t_full = _hyena_tables(L)
    filt = _filter_td(emb_t, t_full, p["w1t"], p["b1"], p["f0"], p["w2t"], p["b2"], p["f1"], p["w3t"],
                      p["absdelta"])
    hf = _filter_fft(filt.reshape(BRANCH_W, n2, FFT_N1), consts, n2, _hyena_chunk(n2))
    return hf, consts, n2


def _layer(x, p, l, lq1, lk1, lq2, lk2, final_g, final):
    B, L, D = x.shape
    T = B * L
    x2 = x.reshape(T, D)
    g = p["g"]
    lam_init = 0.8 - 0.6 * math.exp(-0.3 * l)

    pool_in = _proj(x2, g, p["w_pool"], F32, 512)
    qk = _proj(x2, g, p["w_qk"], BF16, 512).reshape(B, L, 2 * BRANCH_W)
    vt = _proj_t(x2, g, p["wt_v"], BF16, LANES, B, L, True)
    hyt = _proj_t(x2, g, p["wt_hy"], F32, 512, B, L, False)
    cf = _proj(x2, g, p["w_cf"], F32, 512)

    ya = _pool(pool_in, p["pool_w"], p["pool_scale"], L)
    yb = _attention(qk, vt, p["bias"], lq1[l][None, :], lk1[l][None, :], lq2[l][None, :],
                    lk2[l][None, :], p["subln_g"], lam_init).reshape(T, BRANCH_W)
    hf, consts, n2 = _filter_spectrum(p, L)
    yct = _hyena(hyt.reshape(B, 3 * BRANCH_W, n2 // 2, FFT_N1), hf, consts, p["short_w"], p["short_b"], p["hy_d"],
                 n2, _hyena_chunk(n2)).reshape(B, BRANCH_W, L)
    yd = _conformer(cf, p["dw_w"], p["dw_b"], p["ln_g"], p["ln_b"], L)

    out = _merge(x2, ya, yb, yct, yd, g, p["w_silu"], p["w_merge"], p["w_branch"], p["w_out"], final_g[None, :], L,
                 final)
    return out.reshape(B, L, D)


def kernel(x_prompt, x_sample, norm_g, w_in, pool_w, pool_scale, lambda_q1, lambda_k1, lambda_q2, lambda_k2, subln_g, rel_bias, hy_short_w, hy_short_b, hy_w1, hy_b1, hy_freq, hy_w2, hy_b2, hy_w3, hy_d, cf_dw_w, cf_dw_b, cf_ln_g, cf_ln_b, w_branch, w_out, final_g):
    depth = norm_g.shape[0]
    hp, hs = x_prompt, x_sample
    for l in range(depth):
        p = _layer_weights(l, norm_g, w_in, pool_w, pool_scale, subln_g, rel_bias, hy_short_w, hy_short_b, hy_w1,
                           hy_b1, hy_freq, hy_w2, hy_b2, hy_w3, hy_d, cf_dw_w, cf_dw_b, cf_ln_g, cf_ln_b, w_branch,
                           w_out)
        final = l == depth - 1
        hp = _layer(hp, p, l, lambda_q1, lambda_k1, lambda_q2, lambda_k2, final_g, final)
        hs = _layer(hs, p, l, lambda_q1, lambda_k1, lambda_q2, lambda_k2, final_g, final)
    return (hp, hs)
```

```python
import functools
import math

import numpy as np
import jax
import jax.numpy as jnp
from jax import lax
from jax.experimental import pallas as pl
from jax.experimental.pallas import tpu as pltpu

F32 = jnp.float32
BF16 = jnp.bfloat16

D_MODEL = 1024
BRANCH_W = 512
N_BRANCH = 4
POOL_WINDOWS = (2, 4, 8, 16)
POOL_GW = 128
DA_HEADS = 4
DA_HEAD_DIM = 64
DA_V_DIM = 128
REL_BUCKETS = 32
REL_MAX_DIST = 128
HY_EMB = 33
HY_EMB_PAD = 40
HY_BANDS = 16
HY_ORDER = 64
HY_DECAY_TARGET = 1e-2
HY_FAST = 0.3
HY_SLOW = 1.5
CF_WIDTH = 31
EPS = 1e-6

OFF_POOL = 0
OFF_Q = 512
OFF_K = 1024
OFF_V = 1536
OFF_HY = 2048
OFF_CF = 3584
OFF_SILU = 4608
OFF_MERGE = 6656

LANES = 128
SUBLANES = 8
CONF_ROWS = 32
FFT_N1 = 256
HALO = 16
TOK_TILE = 512
ATT_TQ = 256
ATT_TK = 512
ATT_UNROLL = 4
HY_CH_UNROLL = 4
NEG_BIG = -1e30
V_ROWS = DA_V_DIM + 16


def _cparams(sem, vmem_mb, flags=None):
    return pltpu.CompilerParams(dimension_semantics=sem, vmem_limit_bytes=vmem_mb << 20, flags=flags)


def _proj_kernel(x_ref, g_ref, wtok_ref, wchan_ref, pool_ref, qk_ref, cf_ref, vt_ref, hyt_ref):
    x = x_ref[...]
    ms = jnp.mean(x * x, axis=-1, keepdims=True)
    h = (x * lax.rsqrt(ms + EPS) * g_ref[...]).astype(BF16)
    W = BRANCH_W

    def tok(lo, hi):
        return jnp.dot(h, wtok_ref[:, lo:hi], preferred_element_type=F32)

    def chan(lo, hi):
        return lax.dot_general(wchan_ref[lo:hi, :], h, (((1,), (1,)), ((), ())), preferred_element_type=F32)

    pool_ref[...] = tok(0, W)
    qk_ref[...] = tok(W, 3 * W).astype(BF16)
    cf_ref[...] = tok(3 * W, 5 * W)
    for hd in range(DA_HEADS):
        vt_ref[hd, 0, pl.ds(0, DA_V_DIM), :] = chan(hd * DA_V_DIM, (hd + 1) * DA_V_DIM).astype(BF16)
        vt_ref[hd, 0, pl.ds(DA_V_DIM, V_ROWS - DA_V_DIM), :] = jnp.ones((V_ROWS - DA_V_DIM, x.shape[0]), BF16)
    for s in range(3):
        hyt_ref[pl.ds(s * W, W), :] = chan((1 + s) * W, (2 + s) * W)


def _project(x2, g, wtok, wchan, B, L):
    T, D = x2.shape
    tm = ATT_TK
    nl = L // tm
    W = BRANCH_W
    const = lambda a: pl.BlockSpec(a.shape, lambda i: (0,) * a.ndim, pipeline_mode=pl.Buffered(1))
    tok = lambda c: pl.BlockSpec((tm, c), lambda i: (i, 0))
    return pl.pallas_call(
        _proj_kernel,
        grid=(T // tm,),
        in_specs=[tok(D), const(g), const(wtok), const(wchan)],
        out_specs=[tok(W), tok(2 * W), tok(2 * W),
                   pl.BlockSpec((None, DA_HEADS, 1, V_ROWS, tm), lambda i: (i // nl, 0, i % nl, 0, 0)),
                   pl.BlockSpec((None, 3 * W, tm), lambda i: (i // nl, 0, i % nl))],
        out_shape=[jax.ShapeDtypeStruct((T, W), F32), jax.ShapeDtypeStruct((T, 2 * W), BF16),
                   jax.ShapeDtypeStruct((T, 2 * W), F32),
                   jax.ShapeDtypeStruct((B, DA_HEADS, nl, V_ROWS, tm), BF16),
                   jax.ShapeDtypeStruct((B, 3 * W, L), F32)],
        compiler_params=_cparams(("parallel",), 48),
        name="input_projection",
    )(x2, g, wtok, wchan)


def _halo_specs(tm, C, T):
    r = tm // HALO
    last = T // HALO - 1
    return [pl.BlockSpec((tm, C), lambda i: (i, 0)),
            pl.BlockSpec((HALO, C), lambda i: (jnp.maximum(i * r - 1, 0), 0)),
            pl.BlockSpec((HALO, C), lambda i: (jnp.minimum((i + 1) * r, last), 0))]


def _fill_ext(ext_ref, cur_ref, prev_ref, next_ref, tm, L):
    i = pl.program_id(0)
    tiles_per_seq = L // tm
    first = (i % tiles_per_seq) == 0
    lastt = (i % tiles_per_seq) == tiles_per_seq - 1
    ext_ref[pl.ds(0, HALO), :] = jnp.where(first, 0.0, prev_ref[...])
    ext_ref[pl.ds(HALO, tm), :] = cur_ref[...]
    ext_ref[pl.ds(HALO + tm, HALO), :] = jnp.where(lastt, 0.0, next_ref[...])


def _pool_kernel(cur_ref, prev_ref, next_ref, w_ref, sc_ref, o_ref, ext_ref, *, tm, L):
    _fill_ext(ext_ref, cur_ref, prev_ref, next_ref, tm, L)
    i = pl.program_id(0)
    pos = (i % (L // tm)) * tm + lax.broadcasted_iota(jnp.int32, (tm, 1), 0)
    for g, w in enumerate(POOL_WINDOWS):
        lanes = pl.ds(g * POOL_GW, POOL_GW)
        s = None
        for d in range(-(w // 2), w - w // 2):
            t = ext_ref[pl.ds(HALO + d, tm), lanes]
            s = t if s is None else s + t
        lo = jnp.maximum(pos - w // 2, 0)
        hi = jnp.minimum(pos + (w - 1 - w // 2), L - 1)
        cnt = (hi - lo + 1).astype(F32)
        pooled = s / cnt - ext_ref[pl.ds(HALO, tm), lanes]
        mixed = jnp.dot(pooled.astype(BF16), w_ref[g], preferred_element_type=F32)
        o_ref[:, lanes] = mixed * sc_ref[:, lanes]


def _pool(u, w_mix, scale, L):
    T, C = u.shape
    tm = TOK_TILE
    return pl.pallas_call(
        functools.partial(_pool_kernel, tm=tm, L=L),
        grid=(T // tm,),
        in_specs=_halo_specs(tm, C, T) + [
            pl.BlockSpec((4, POOL_GW, POOL_GW), lambda i: (0, 0, 0)),
            pl.BlockSpec((1, C), lambda i: (0, 0))],
        out_specs=pl.BlockSpec((tm, C), lambda i: (i, 0)),
        out_shape=jax.ShapeDtypeStruct((T, C), F32),
        scratch_shapes=[pltpu.VMEM((tm + 2 * HALO, C), F32)],
        compiler_params=_cparams(("parallel",), 32),
        name="pool_mixer",
    )(u, u, u, w_mix, scale)


def _conf_kernel(cur_ref, prev_ref, next_ref, w_ref, b_ref, g_ref, beta_ref, o_ref, ext_ref, hs_ref, *, tm, L):
    _fill_ext(ext_ref, cur_ref, prev_ref, next_ref, tm, L)
    a = ext_ref[:, pl.ds(0, BRANCH_W)]
    gt = ext_ref[:, pl.ds(BRANCH_W, BRANCH_W)]
    hs_ref[0] = a * jax.nn.sigmoid(gt)
    shifted_rows = tm + 2 * HALO - SUBLANES
    for b in range(1, SUBLANES):
        hs_ref[b, pl.ds(0, shifted_rows), :] = hs_ref[0, pl.ds(b, shifted_rows), :]

    def block(rb, carry):
        r0 = rb * CONF_ROWS
        grp = (CONF_ROWS // SUBLANES, SUBLANES, BRANCH_W)
        acc = jnp.zeros(grp, F32)
        for j in range(CF_WIDTH):
            a8, b = divmod(HALO - CF_WIDTH // 2 + j, SUBLANES)
            start = pl.multiple_of(r0 + SUBLANES * a8, SUBLANES)
            acc = acc + w_ref[j] * hs_ref[b, pl.ds(start, CONF_ROWS), :].reshape(grp)
        o_ref[pl.ds(pl.multiple_of(r0, CONF_ROWS), CONF_ROWS), :] = acc.reshape(CONF_ROWS, BRANCH_W) + b_ref[...]
        return carry

    lax.fori_loop(0, tm // CONF_ROWS, block, 0, unroll=2)
    acc = o_ref[...]
    mu = jnp.mean(acc, axis=-1, keepdims=True)
    cen = acc - mu
    var = jnp.mean(cen * cen, axis=-1, keepdims=True)
    y = cen * lax.rsqrt(var + EPS) * g_ref[...] + beta_ref[...]
    o_ref[...] = y * jax.nn.sigmoid(y)


def _conformer(u, dw_w, dw_b, ln_g, ln_b, L):
    T, C2 = u.shape
    C = C2 // 2
    tm = TOK_TILE
    vec = pl.BlockSpec((1, C), lambda i: (0, 0))
    return pl.pallas_call(
        functools.partial(_conf_kernel, tm=tm, L=L),
        grid=(T // tm,),
        in_specs=_halo_specs(tm, C2, T) + [pl.BlockSpec((CF_WIDTH, SUBLANES, C), lambda i: (0, 0, 0)), vec, vec, vec],
        out_specs=pl.BlockSpec((tm, C), lambda i: (i, 0)),
        out_shape=jax.ShapeDtypeStruct((T, C), F32),
        scratch_shapes=[pltpu.VMEM((tm + 2 * HALO, C2), F32), pltpu.VMEM((SUBLANES, tm + 2 * HALO, C), F32)],
        compiler_params=_cparams(("parallel",), 40),
        name="conformer_conv",
    )(u, u, u, dw_w, dw_b, ln_g, ln_b)


def _rel_bucket_np(rel):
    nb = REL_BUCKETS // 2
    max_exact = nb // 2
    ret = np.where(rel > 0, nb, 0)
    n = np.abs(rel)
    ratio = np.log(np.maximum(n, 1).astype(np.float32) / np.float32(max_exact)) / np.float32(
        math.log(REL_MAX_DIST / max_exact))
    large = max_exact + (ratio * np.float32(nb - max_exact)).astype(np.int32)
    large = np.minimum(large, nb - 1)
    return ret + np.where(n < max_exact, n, large)


def _bias_buckets():
    r = np.arange(ATT_TK)[:, None]
    c = np.arange(ATT_TQ)[None, :]
    out = np.zeros((2, 5, ATT_TK, ATT_TQ), np.int32)
    for e in range(2):
        for dj in range(-2, 3):
            out[e, dj + 2] = _rel_bucket_np(ATT_TK * dj + r - ATT_TQ * e - c)
    assert (out[:, 0] == REL_BUCKETS // 2 - 1).all() and (out[:, 4] == REL_BUCKETS - 1).all()
    return out


def _attn_kernel(q_ref, k_ref, vt_ref, bias_ref, lq1_ref, lk1_ref, lq2_ref, lk2_ref, sg_ref,
                 o_ref, qm_ref, s_ref, p_ref, al_ref, m_ref, acc_ref, *, nk, lam_init):
    qi = pl.program_id(2)
    q = q_ref[...]
    lane = lax.broadcasted_iota(jnp.int32, q.shape, 1)
    qs = q * jnp.asarray(DA_HEAD_DIM ** -0.5, BF16)
    zero = jnp.zeros_like(qs)
    qm_ref[0] = jnp.where(lane < DA_HEAD_DIM, qs, zero)
    qm_ref[1] = jnp.where(lane >= DA_HEAD_DIM, qs, zero)
    m_ref[...] = jnp.full(m_ref.shape, NEG_BIG, F32)
    acc_ref[...] = jnp.zeros(acc_ref.shape, F32)
    p_ref[1] = jnp.zeros(p_ref.shape[1:], BF16)
    al_ref[1] = jnp.ones(al_ref.shape[1:], F32)
    j0 = qi // 2
    e = qi % 2

    def scores(j, slot):
        kblk = k_ref[pl.ds(pl.multiple_of(j * ATT_TK, ATT_TK), ATT_TK), :]
        kind = jnp.clip(j - j0, -2, 2) + 2
        for mm in range(2):
            s = lax.dot_general(kblk, qm_ref[mm], (((1,), (1,)), ((), ())), preferred_element_type=F32)
            s_ref[slot, mm] = s + bias_ref[e, kind]

    def softmax(slot):
        for mm in range(2):
            m_prev = m_ref[mm]
            m_new = jnp.maximum(m_prev, jnp.max(s_ref[slot, mm], axis=0, keepdims=True))
            al_ref[slot, mm] = jnp.exp(m_prev - m_new)
            p_ref[slot, mm] = jnp.exp(s_ref[slot, mm] - m_new).astype(BF16)
            m_ref[mm] = m_new

    def weighted_values(j, slot):
        vblk = vt_ref[j]
        for mm in range(2):
            acc_ref[mm] = al_ref[slot, mm] * acc_ref[mm] + jnp.dot(vblk, p_ref[slot, mm],
                                                                   preferred_element_type=F32)

    scores(0, 0)

    def body(i, carry):
        for u in range(ATT_UNROLL):
            j = ATT_UNROLL * i + u
            cur = u % 2
            scores(jnp.minimum(j + 1, nk - 1), 1 - cur)
            softmax(cur)
            weighted_values(jnp.maximum(j - 1, 0), 1 - cur)
        return carry

    lax.fori_loop(0, nk // ATT_UNROLL, body, 0)
    weighted_values(nk - 1, 1)

    lam = (jnp.exp(jnp.sum(lq1_ref[...] * lk1_ref[...], keepdims=True))
           - jnp.exp(jnp.sum(lq2_ref[...] * lk2_ref[...], keepdims=True)) + lam_init)
    a0 = acc_ref[0]
    a1 = acc_ref[1]
    o = (a0[:DA_V_DIM] / a0[DA_V_DIM:DA_V_DIM + 1]
         - lam * (a1[:DA_V_DIM] / a1[DA_V_DIM:DA_V_DIM + 1]))
    ms = jnp.mean(o * o, axis=0, keepdims=True)
    y = o * lax.rsqrt(ms + EPS) * sg_ref[...] * (1.0 - lam_init)
    o_ref[...] = y.T


def _attention(qk, vt, bias, lq1, lk1, lq2, lk2, sg, lam_init):
    B, L, _ = qk.shape
    nk = L // ATT_TK
    assert nk % ATT_UNROLL == 0 and ATT_UNROLL % 2 == 0
    vec = pl.BlockSpec((1, DA_HEAD_DIM), lambda b, h, i: (0, 0))
    return pl.pallas_call(
        functools.partial(_attn_kernel, nk=nk, lam_init=lam_init),
        grid=(B, DA_HEADS, L // ATT_TQ),
        in_specs=[pl.BlockSpec((None, ATT_TQ, LANES), lambda b, h, i: (b, i, h)),
                  pl.BlockSpec((None, L, LANES), lambda b, h, i: (b, 0, DA_HEADS + h)),
                  pl.BlockSpec((None, None, nk, V_ROWS, ATT_TK), lambda b, h, i: (b, h, 0, 0, 0)),
                  pl.BlockSpec((None, 2, 5, ATT_TK, ATT_TQ), lambda b, h, i: (h, 0, 0, 0, 0)),
                  vec, vec, vec, vec,
                  pl.BlockSpec((DA_V_DIM, 1), lambda b, h, i: (0, 0))],
        out_specs=pl.BlockSpec((None, ATT_TQ, LANES), lambda b, h, i: (b, i, h)),
        out_shape=jax.ShapeDtypeStruct((B, L, BRANCH_W), F32),
        scratch_shapes=[pltpu.VMEM((2, ATT_TQ, LANES), BF16),
                        pltpu.VMEM((2, 2, ATT_TK, ATT_TQ), F32),
                        pltpu.VMEM((2, 2, ATT_TK, ATT_TQ), BF16),
                        pltpu.VMEM((2, 2, 1, ATT_TQ), F32),
                        pltpu.VMEM((2, 1, ATT_TQ), F32),
                        pltpu.VMEM((2, V_ROWS, ATT_TQ), F32)],
        compiler_params=_cparams(("parallel", "parallel", "arbitrary"), 52),
        name="diff_attention",
    )(qk, qk, vt, bias, lq1, lk1, lq2, lk2, sg)


def _fft_consts(n2):
    n1 = FFT_N1
    n = n1 * n2
    f2 = np.exp(-2j * np.pi * np.outer(np.arange(n2), np.arange(n2)) / n2)
    tw = np.exp(-2j * np.pi * np.outer(np.arange(n2), np.arange(n1)) / n)
    f1 = np.exp(-2j * np.pi * np.outer(np.arange(n1), np.arange(n1)) / n1)
    c = {}
    c["f2_full"] = np.concatenate([f2.real, f2.imag], 0)
    c["f2_half"] = c["f2_full"][:, : n2 // 2]
    c["tre"], c["tim"] = tw.real, tw.imag
    c["g_top"] = np.concatenate([f1.real, f1.imag], 1)
    c["g_bot"] = np.concatenate([-f1.imag, f1.real], 1)
    c["gi_top"] = np.concatenate([f1.real, -f1.imag], 1)
    c["gi_bot"] = np.concatenate([f1.imag, f1.real], 1)
    c["fin_re"] = f2.real.T[: n2 // 2]
    c["fin_im"] = f2.imag.T[: n2 // 2]
    out = {}
    for k, v in c.items():
        out[k] = jnp.asarray(v, F32 if k in ("tre", "tim") else BF16)
    return out


def _filter_td_kernel(emb_ref, t_ref, w1_ref, b1_ref, f0_ref, w2_ref, b2_ref, f1_ref, w3_ref, ad_ref, o_ref):
    emb = emb_ref[...].astype(BF16)
    h = jnp.dot(w1_ref[...], emb, preferred_element_type=F32) + b1_ref[...]
    h = jnp.sin(f0_ref[...] * h)
    h = jnp.dot(w2_ref[...], h.astype(BF16), preferred_element_type=F32) + b2_ref[...]
    h = jnp.sin(f1_ref[...] * h)
    h = jnp.dot(w3_ref[...], h.astype(BF16), preferred_element_type=F32)
    decay = jnp.exp(-(ad_ref[...] * t_ref[...]))
    o_ref[...] = h * decay


def _filter_td(emb_t, t_full, w1t, b1, f0, w2t, b2, f1, w3t, absdelta):
    n = emb_t.shape[1]
    pt = 2048
    nch = n // pt
    col = lambda r: pl.BlockSpec((r, 1), lambda i: (0, 0))
    return pl.pallas_call(
        _filter_td_kernel,
        grid=(nch,),
        in_specs=[pl.BlockSpec((HY_EMB_PAD, pt), lambda i: (0, i)),
                  pl.BlockSpec((1, pt), lambda i: (0, i)),
                  pl.BlockSpec((HY_ORDER, HY_EMB_PAD), lambda i: (0, 0)), col(HY_ORDER), col(HY_ORDER),
                  pl.BlockSpec((HY_ORDER, HY_ORDER), lambda i: (0, 0)), col(HY_ORDER), col(HY_ORDER),
                  pl.BlockSpec((None, BRANCH_W, HY_ORDER), lambda i: (i // (nch // 2), 0, 0)),
                  col(BRANCH_W)],
        out_specs=pl.BlockSpec((BRANCH_W, pt), lambda i: (0, i)),
        out_shape=jax.ShapeDtypeStruct((BRANCH_W, n), F32),
        compiler_params=_cparams(("parallel",), 32),
        name="hyena_filter_time",
    )(emb_t, t_full, w1t, b1, f0, w2t, b2, f1, w3t, absdelta)


def _cmul(are, aim, bre, bim):
    return are * bre - aim * bim, are * bim + aim * bre


def _filter_fft_kernel(f_ref, f2_ref, tre_ref, tim_ref, gt_ref, gb_ref, o_ref, bre_ref, bim_ref, *, cc, n2):
    n_total = FFT_N1 * n2
    row = lax.broadcasted_iota(jnp.int32, (n2, FFT_N1), 0)
    lane = lax.broadcasted_iota(jnp.int32, (n2, FFT_N1), 1)
    is_lag0_backward = (row == n2 // 2) & (lane == 0)

    def per_channel(c, carry):
        f = f_ref[c]
        s = jnp.sum(jnp.abs(f), keepdims=True)
        f = jnp.where(is_lag0_backward, 0.0, f) * (1.0 / ((s + EPS) * n_total))
        a = jnp.dot(f2_ref[...], f.astype(BF16), preferred_element_type=F32)
        bre, bim = _cmul(a[:n2], a[n2:], tre_ref[...], tim_ref[...])
        bre_ref[c] = bre.astype(BF16)
        bim_ref[c] = bim.astype(BF16)
        return carry

    lax.fori_loop(0, cc, per_channel, 0, unroll=HY_CH_UNROLL)
    bre = bre_ref[...].reshape(cc * n2, FFT_N1)
    bim = bim_ref[...].reshape(cc * n2, FFT_N1)
    o_ref[...] = (jnp.dot(bre, gt_ref[...], preferred_element_type=F32)
                  + jnp.dot(bim, gb_ref[...], preferred_element_type=F32))


def _filter_fft(filt, consts, n2, cc):
    C = filt.shape[0]
    full = lambda a: pl.BlockSpec(a.shape, lambda i: (0,) * a.ndim)
    mats = [consts["f2_full"], consts["tre"], consts["tim"], consts["g_top"], consts["g_bot"]]
    return pl.pallas_call(
        functools.partial(_filter_fft_kernel, cc=cc, n2=n2),
        grid=(C // cc,),
        in_specs=[pl.BlockSpec((cc, n2, FFT_N1), lambda i: (i, 0, 0))] + [full(a) for a in mats],
        out_specs=pl.BlockSpec((cc * n2, 2 * FFT_N1), lambda i: (i, 0)),
        out_shape=jax.ShapeDtypeStruct((C * n2, 2 * FFT_N1), F32),
        scratch_shapes=[pltpu.VMEM((cc, n2, FFT_N1), BF16), pltpu.VMEM((cc, n2, FFT_N1), BF16)],
        compiler_params=_cparams(("parallel",), 40),
        name="hyena_filter_fft",
    )(filt, *mats)


def _shift_prev(x, row, lane):
    a = pltpu.roll(x, 1, axis=1)
    b = pltpu.roll(a, 1, axis=0)
    return jnp.where(lane == 0, jnp.where(row == 0, 0.0, b), a)


def _shift_next(x, row, lane, rows):
    a = pltpu.roll(x, FFT_N1 - 1, axis=1)
    b = pltpu.roll(a, rows - 1, axis=0)
    return jnp.where(lane == FFT_N1 - 1, jnp.where(row == rows - 1, 0.0, b), a)


def _hyena_kernel(sw_ref, sb_ref, d_ref, x0_ref, x1_ref, v_ref, hf_ref, f2_ref, tre_ref, tim_ref,
                  gt_ref, gb_ref, git_ref, gib_ref, fre_ref, fim_ref, o_ref,
                  z_ref, x0c_ref, bre_ref, bim_ref, c_ref, *, cc, n2):
    rows = n2 // 2
    ch0 = pl.program_id(1) * cc
    row = lax.broadcasted_iota(jnp.int32, (rows, FFT_N1), 0)
    lane = lax.broadcasted_iota(jnp.int32, (rows, FFT_N1), 1)

    def short_conv(x, ch):
        return (sw_ref[0, ch] * _shift_prev(x, row, lane) + sw_ref[1, ch] * x
                + sw_ref[2, ch] * _shift_next(x, row, lane, rows) + sb_ref[ch])

    def forward(c, carry):
        ch = ch0 + c
        x0c_ref[c] = short_conv(x0_ref[c], ch)
        z = short_conv(x1_ref[c], BRANCH_W + ch) * short_conv(v_ref[c], 2 * BRANCH_W + ch)
        z_ref[c] = z
        a = jnp.dot(f2_ref[...], z.astype(BF16), preferred_element_type=F32)
        bre, bim = _cmul(a[:n2], a[n2:], tre_ref[...], tim_ref[...])
        bre_ref[c] = bre.astype(BF16)
        bim_ref[c] = bim.astype(BF16)
        return carry

    lax.fori_loop(0, cc, forward, 0, unroll=HY_CH_UNROLL)

    bre = bre_ref[...].reshape(cc * n2, FFT_N1)
    bim = bim_ref[...].reshape(cc * n2, FFT_N1)
    x = (jnp.dot(bre, gt_ref[...], preferred_element_type=F32)
         + jnp.dot(bim, gb_ref[...], preferred_element_type=F32))
    hf = hf_ref[...]
    yre, yim = _cmul(x[:, :FFT_N1], x[:, FFT_N1:], hf[:, :FFT_N1], hf[:, FFT_N1:])
    cfull = (jnp.dot(yre.astype(BF16), git_ref[...], preferred_element_type=F32)
             + jnp.dot(yim.astype(BF16), gib_ref[...], preferred_element_type=F32))
    c_ref[...] = cfull.reshape(cc, n2, 2 * FFT_N1)

    def backward(c, carry):
        cm = c_ref[c]
        cre, cim = cm[:, :FFT_N1], cm[:, FFT_N1:]
        tre, tim = tre_ref[...], tim_ref[...]
        dre = cre * tre + cim * tim
        dim = cim * tre - cre * tim
        y = (jnp.dot(fre_ref[...], dre.astype(BF16), preferred_element_type=F32)
             + jnp.dot(fim_ref[...], dim.astype(BF16), preferred_element_type=F32))
        o_ref[c] = x0c_ref[c] * (y + z_ref[c] * d_ref[ch0 + c])
        return carry

    lax.fori_loop(0, cc, backward, 0, unroll=HY_CH_UNROLL)


def _hyena(u, hf, consts, short_w, short_b, d_bias, n2, cc):
    B = u.shape[0]
    rows = n2 // 2
    C = BRANCH_W
    nblk = C // cc
    smem = pl.BlockSpec(memory_space=pltpu.SMEM)
    full = lambda a: pl.BlockSpec(a.shape, lambda b, i: (0,) * a.ndim)
    mats = [consts[k] for k in ("f2_half", "tre", "tim", "g_top", "g_bot", "gi_top", "gi_bot", "fin_re", "fin_im")]
    sec = lambda s: pl.BlockSpec((None, cc, rows, FFT_N1), lambda b, i: (b, s * nblk + i, 0, 0))
    return pl.pallas_call(
        functools.partial(_hyena_kernel, cc=cc, n2=n2),
        grid=(B, nblk),
        in_specs=[smem, smem, smem, sec(0), sec(1), sec(2),
                  pl.BlockSpec((cc * n2, 2 * FFT_N1), lambda b, i: (i, 0))] + [full(a) for a in mats],
        out_specs=pl.BlockSpec((None, cc, rows, FFT_N1), lambda b, i: (b, i, 0, 0)),
        out_shape=jax.ShapeDtypeStruct((B, C, rows, FFT_N1), F32),
        scratch_shapes=[pltpu.VMEM((cc, rows, FFT_N1), F32), pltpu.VMEM((cc, rows, FFT_N1), F32),
                        pltpu.VMEM((cc, n2, FFT_N1), BF16), pltpu.VMEM((cc, n2, FFT_N1), BF16),
                        pltpu.VMEM((cc, n2, 2 * FFT_N1), F32)],
        compiler_params=_cparams(("parallel", "arbitrary"), 40),
        name="hyena_mixer",
    )(short_w, short_b, d_bias, u, u, u, hf, *mats)


def _hyena_tables(L):
    t = jnp.linspace(0.0, 1.0, L, dtype=F32)[:, None]
    w = 2.0 * math.pi * jnp.arange(L, dtype=F32)[:, None] / L
    bands = jnp.linspace(1e-4, HY_BANDS - 1, HY_BANDS, dtype=F32)[None, :]
    emb = jnp.concatenate([t, jnp.cos(bands * w), -jnp.sin(bands * w)], axis=-1)
    two_sided = lambda a: jnp.concatenate([a, a[:1], a[:0:-1]], axis=0)
    emb_full = jnp.pad(two_sided(emb), ((0, 0), (0, HY_EMB_PAD - HY_EMB)))
    t_full = two_sided(t).T
    return emb_full.T, t_full


def _merge_kernel(x_ref, ya_ref, yb_ref, yct_ref, yd_ref, g_ref, ws_ref, wm_ref, wb_ref, wo_ref, fg_ref, o_ref,
                  *, final):
    x = x_ref[...]
    ms = jnp.mean(x * x, axis=-1, keepdims=True)
    h = (x * lax.rsqrt(ms + EPS) * g_ref[...]).astype(BF16)
    mixed = None
    for n in range(N_BRANCH):
        if n == 0:
            y = ya_ref[...]
        elif n == 1:
            y = yb_ref[...]
        elif n == 2:
            y = yct_ref[...].T
        else:
            y = yd_ref[...]
        sg = jnp.dot(h, ws_ref[n], preferred_element_type=F32)
        u = (y * (sg * jax.nn.sigmoid(sg))).astype(BF16)
        proj = jnp.dot(u, wb_ref[n], preferred_element_type=F32)
        merge = jax.nn.sigmoid(jnp.dot(h, wm_ref[n], preferred_element_type=F32))
        mixed = merge * proj if mixed is None else mixed + merge * proj
    out = x + jnp.dot(mixed.astype(BF16), wo_ref[...], preferred_element_type=F32)
    if final:
        ms = jnp.mean(out * out, axis=-1, keepdims=True)
        out = out * lax.rsqrt(ms + EPS) * fg_ref[...]
    o_ref[...] = out


def _merge(x2, ya, yb, yct, yd, g, ws, wm, wb, wo, fg, L, final):
    T, D = x2.shape
    tm = 256
    nl = L // tm
    W = BRANCH_W
    tok = lambda c: pl.BlockSpec((tm, c), lambda i: (i, 0))
    const = lambda a: pl.BlockSpec(a.shape, lambda i: (0,) * a.ndim, pipeline_mode=pl.Buffered(1))
    return pl.pallas_call(
        functools.partial(_merge_kernel, final=final),
        grid=(T // tm,),
        in_specs=[tok(D), tok(W), tok(W),
                  pl.BlockSpec((None, W, tm), lambda i: (i // nl, 0, i % nl)),
                  tok(W), const(g), const(ws), const(wm), const(wb), const(wo), const(fg)],
        out_specs=tok(D),
        out_shape=jax.ShapeDtypeStruct((T, D), F32),
        compiler_params=_cparams(("parallel",), 52),
        name="gated_merge",
    )(x2, ya, yb, yct, yd, g, ws, wm, wb, wo, fg)


def _layer_weights(l, norm_g, w_in, pool_w, pool_scale, subln_g, rel_bias, hy_short_w, hy_short_b, hy_w1, hy_b1,
                   hy_freq, hy_w2, hy_b2, hy_w3, hy_d, cf_dw_w, cf_dw_b, cf_ln_g, cf_ln_b, w_branch, w_out):
    w = w_in[l].astype(BF16)
    W = BRANCH_W
    p = {}
    p["g"] = norm_g[l][None, :]
    p["w_tok"] = jnp.concatenate([w[:, OFF_POOL:OFF_V], w[:, OFF_CF:OFF_SILU]], axis=1)
    p["w_chan"] = w[:, OFF_V:OFF_CF].T
    p["w_silu"] = w[:, OFF_SILU:OFF_MERGE].reshape(D_MODEL, N_BRANCH, W).transpose(1, 0, 2)
    p["w_merge"] = w[:, OFF_MERGE:].reshape(D_MODEL, N_BRANCH, D_MODEL).transpose(1, 0, 2)
    p["w_branch"] = w_branch[l].astype(BF16)
    p["w_out"] = w_out[l].astype(BF16)
    p["pool_w"] = pool_w[l].astype(BF16)
    p["pool_scale"] = pool_scale[l][None, :]
    p["subln_g"] = subln_g[l][:, None]
    buckets = jnp.asarray(_bias_buckets())[None]
    table_t = rel_bias.astype(F32).T
    bias = jnp.zeros((DA_HEADS,) + buckets.shape[1:], F32)
    for b in range(REL_BUCKETS):
        bias = jnp.where(buckets == b, table_t[:, b][:, None, None, None, None], bias)
    p["bias"] = bias
    p["short_w"] = hy_short_w[l]
    p["short_b"] = hy_short_b[l]
    p["hy_d"] = hy_d[l]
    p["w1t"] = jnp.pad(hy_w1[l].T, ((0, 0), (0, HY_EMB_PAD - HY_EMB))).astype(BF16)
    p["b1"] = hy_b1[l][:, None]
    p["f0"] = hy_freq[l, 0][:, None]
    p["w2t"] = hy_w2[l].T.astype(BF16)
    p["b2"] = hy_b2[l][:, None]
    p["f1"] = hy_freq[l, 1][:, None]
    p["w3t"] = hy_w3[l].T.reshape(2, W, HY_ORDER).astype(BF16)
    max_decay = math.log(HY_DECAY_TARGET) / HY_FAST
    min_decay = math.log(HY_DECAY_TARGET) / HY_SLOW
    p["absdelta"] = jnp.abs(jnp.linspace(min_decay, max_decay, W, dtype=F32))[:, None]
    p["dw_w"] = jnp.broadcast_to(cf_dw_w[l][:, None, :], (CF_WIDTH, SUBLANES, BRANCH_W))
    p["dw_b"] = cf_dw_b[l][None, :]
    p["ln_g"] = cf_ln_g[l][None, :]
    p["ln_b"] = cf_ln_b[l][None, :]
    return p


def _hyena_chunk(n2):
    return max(1, 512 // n2)


def _filter_spectrum(p, L):
    n2 = 2 * L // FFT_N1
    consts = _fft_consts(n2)
    emb_t, t_full = _hyena_tables(L)
    filt = _filter_td(emb_t, t_full, p["w1t"], p["b1"], p["f0"], p["w2t"], p["b2"], p["f1"], p["w3t"],
                      p["absdelta"])
    hf = _filter_fft(filt.reshape(BRANCH_W, n2, FFT_N1), consts, n2, _hyena_chunk(n2))
    return hf, consts, n2


def _layer(x, p, l, lq1, lk1, lq2, lk2, final_g, final):
    B, L, D = x.shape
    T = B * L
    x2 = x.reshape(T, D)
    g = p["g"]
    lam_init = 0.8 - 0.6 * math.exp(-0.3 * l)

    pool_in, qk, cf, vt, hyt = _project(x2, g, p["w_tok"], p["w_chan"], B, L)
    qk = qk.reshape(B, L, 2 * BRANCH_W)

    ya = _pool(pool_in, p["pool_w"], p["pool_scale"], L)
    yb = _attention(qk, vt, p["bias"], lq1[l][None, :], lk1[l][None, :], lq2[l][None, :],
                    lk2[l][None, :], p["subln_g"], lam_init).reshape(T, BRANCH_W)
    hf, consts, n2 = _filter_spectrum(p, L)
    yct = _hyena(hyt.reshape(B, 3 * BRANCH_W, n2 // 2, FFT_N1), hf, consts, p["short_w"], p["short_b"], p["hy_d"],
                 n2, _hyena_chunk(n2)).reshape(B, BRANCH_W, L)
    yd = _conformer(cf, p["dw_w"], p["dw_b"], p["ln_g"], p["ln_b"], L)

    out = _merge(x2, ya, yb, yct, yd, g, p["w_silu"], p["w_merge"], p["w_branch"], p["w_out"], final_g[None, :], L,
                 final)
    return out.reshape(B, L, D)


def kernel(x_prompt, x_sample, norm_g, w_in, pool_w, pool_scale, lambda_q1, lambda_k1, lambda_q2, lambda_k2, subln_g, rel_bias, hy_short_w, hy_short_b, hy_w1, hy_b1, hy_freq, hy_w2, hy_b2, hy_w3, hy_d, cf_dw_w, cf_dw_b, cf_ln_g, cf_ln_b, w_branch, w_out, final_g):
    depth = norm_g.shape[0]
    hp, hs = x_prompt, x_sample
    for l in range(depth):
        p = _layer_weights(l, norm_g, w_in, pool_w, pool_scale, subln_g, rel_bias, hy_short_w, hy_short_b, hy_w1,
                           hy_b1, hy_freq, hy_w2, hy_b2, hy_w3, hy_d, cf_dw_w, cf_dw_b, cf_ln_g, cf_ln_b, w_branch,
                           w_out)
        final = l == depth - 1
        hp = _layer(hp, p, l, lambda_q1, lambda_k1, lambda_q2, lambda_k2, final_g, final)
        hs = _layer(hs, p, l, lambda_q1, lambda_k1, lambda_q2, lambda_k2, final_g, final)
    return (hp, hs)
```

```python
import functools
import math

import numpy as np
import jax
import jax.numpy as jnp
from jax import lax
from jax.experimental import pallas as pl
from jax.experimental.pallas import tpu as pltpu

F32 = jnp.float32
BF16 = jnp.bfloat16

D_MODEL = 1024
BRANCH_W = 512
N_BRANCH = 4
POOL_WINDOWS = (2, 4, 8, 16)
POOL_GW = 128
DA_HEADS = 4
DA_HEAD_DIM = 64
DA_V_DIM = 128
REL_BUCKETS = 32
REL_MAX_DIST = 128
HY_EMB = 33
HY_EMB_PAD = 40
HY_BANDS = 16
HY_ORDER = 64
HY_DECAY_TARGET = 1e-2
HY_FAST = 0.3
HY_SLOW = 1.5
CF_WIDTH = 31
EPS = 1e-6

OFF_POOL = 0
OFF_Q = 512
OFF_K = 1024
OFF_V = 1536
OFF_HY = 2048
OFF_CF = 3584
OFF_SILU = 4608
OFF_MERGE = 6656

LANES = 128
SUBLANES = 8
CONF_ROWS = 32
FFT_N1 = 256
HALO = 16
TOK_TILE = 512
ATT_TQ = 256
ATT_TK = 512
ATT_UNROLL = 4
HY_CH_UNROLL = 4
NEG_BIG = -1e30
V_ROWS = DA_V_DIM + 16


def _cparams(sem, vmem_mb, flags=None):
    return pltpu.CompilerParams(dimension_semantics=sem, vmem_limit_bytes=vmem_mb << 20, flags=flags)


def _proj_kernel(x_ref, g_ref, wtok_ref, wchan_ref, pool_ref, qk_ref, cf_ref, vt_ref, hyt_ref):
    x = x_ref[...]
    ms = jnp.mean(x * x, axis=-1, keepdims=True)
    h = (x * lax.rsqrt(ms + EPS) * g_ref[...]).astype(BF16)
    W = BRANCH_W

    def tok(lo, hi):
        return jnp.dot(h, wtok_ref[:, lo:hi], preferred_element_type=F32)

    def chan(lo, hi):
        return lax.dot_general(wchan_ref[lo:hi, :], h, (((1,), (1,)), ((), ())), preferred_element_type=F32)

    pool_ref[...] = tok(0, W)
    qk_ref[...] = tok(W, 3 * W).astype(BF16)
    cf_ref[...] = tok(3 * W, 5 * W)
    for hd in range(DA_HEADS):
        vt_ref[hd, 0, pl.ds(0, DA_V_DIM), :] = chan(hd * DA_V_DIM, (hd + 1) * DA_V_DIM).astype(BF16)
        vt_ref[hd, 0, pl.ds(DA_V_DIM, V_ROWS - DA_V_DIM), :] = jnp.ones((V_ROWS - DA_V_DIM, x.shape[0]), BF16)
    for s in range(3):
        hyt_ref[pl.ds(s * W, W), :] = chan((1 + s) * W, (2 + s) * W)


def _project(x2, g, wtok, wchan, B, L):
    T, D = x2.shape
    tm = ATT_TK
    nl = L // tm
    W = BRANCH_W
    const = lambda a: pl.BlockSpec(a.shape, lambda i: (0,) * a.ndim, pipeline_mode=pl.Buffered(1))
    tok = lambda c: pl.BlockSpec((tm, c), lambda i: (i, 0))
    return pl.pallas_call(
        _proj_kernel,
        grid=(T // tm,),
        in_specs=[tok(D), const(g), const(wtok), const(wchan)],
        out_specs=[tok(W), tok(2 * W), tok(2 * W),
                   pl.BlockSpec((None, DA_HEADS, 1, V_ROWS, tm), lambda i: (i // nl, 0, i % nl, 0, 0)),
                   pl.BlockSpec((None, 3 * W, tm), lambda i: (i // nl, 0, i % nl))],
        out_shape=[jax.ShapeDtypeStruct((T, W), F32), jax.ShapeDtypeStruct((T, 2 * W), BF16),
                   jax.ShapeDtypeStruct((T, 2 * W), F32),
                   jax.ShapeDtypeStruct((B, DA_HEADS, nl, V_ROWS, tm), BF16),
                   jax.ShapeDtypeStruct((B, 3 * W, L), F32)],
        compiler_params=_cparams(("parallel",), 48),
        name="input_projection",
    )(x2, g, wtok, wchan)


def _halo_specs(tm, C, T):
    r = tm // HALO
    last = T // HALO - 1
    return [pl.BlockSpec((tm, C), lambda i: (i, 0)),
            pl.BlockSpec((HALO, C), lambda i: (jnp.maximum(i * r - 1, 0), 0)),
            pl.BlockSpec((HALO, C), lambda i: (jnp.minimum((i + 1) * r, last), 0))]


def _fill_ext(ext_ref, cur_ref, prev_ref, next_ref, tm, L):
    i = pl.program_id(0)
    tiles_per_seq = L // tm
    first = (i % tiles_per_seq) == 0
    lastt = (i % tiles_per_seq) == tiles_per_seq - 1
    ext_ref[pl.ds(0, HALO), :] = jnp.where(first, 0.0, prev_ref[...])
    ext_ref[pl.ds(HALO, tm), :] = cur_ref[...]
    ext_ref[pl.ds(HALO + tm, HALO), :] = jnp.where(lastt, 0.0, next_ref[...])


def _pool_kernel(cur_ref, prev_ref, next_ref, w_ref, sc_ref, o_ref, ext_ref, *, tm, L):
    _fill_ext(ext_ref, cur_ref, prev_ref, next_ref, tm, L)
    i = pl.program_id(0)
    pos = (i % (L // tm)) * tm + lax.broadcasted_iota(jnp.int32, (tm, 1), 0)
    for g, w in enumerate(POOL_WINDOWS):
        lanes = pl.ds(g * POOL_GW, POOL_GW)
        s = None
        for d in range(-(w // 2), w - w // 2):
            t = ext_ref[pl.ds(HALO + d, tm), lanes]
            s = t if s is None else s + t
        lo = jnp.maximum(pos - w // 2, 0)
        hi = jnp.minimum(pos + (w - 1 - w // 2), L - 1)
        cnt = (hi - lo + 1).astype(F32)
        pooled = s / cnt - ext_ref[pl.ds(HALO, tm), lanes]
        mixed = jnp.dot(pooled.astype(BF16), w_ref[g], preferred_element_type=F32)
        o_ref[:, lanes] = mixed * sc_ref[:, lanes]


def _pool(u, w_mix, scale, L):
    T, C = u.shape
    tm = TOK_TILE
    return pl.pallas_call(
        functools.partial(_pool_kernel, tm=tm, L=L),
        grid=(T // tm,),
        in_specs=_halo_specs(tm, C, T) + [
            pl.BlockSpec((4, POOL_GW, POOL_GW), lambda i: (0, 0, 0)),
            pl.BlockSpec((1, C), lambda i: (0, 0))],
        out_specs=pl.BlockSpec((tm, C), lambda i: (i, 0)),
        out_shape=jax.ShapeDtypeStruct((T, C), F32),
        scratch_shapes=[pltpu.VMEM((tm + 2 * HALO, C), F32)],
        compiler_params=_cparams(("parallel",), 32),
        name="pool_mixer",
    )(u, u, u, w_mix, scale)


def _conf_kernel(cur_ref, prev_ref, next_ref, w_ref, b_ref, g_ref, beta_ref, o_ref, ext_ref, hs_ref, *, tm, L):
    _fill_ext(ext_ref, cur_ref, prev_ref, next_ref, tm, L)
    a = ext_ref[:, pl.ds(0, BRANCH_W)]
    gt = ext_ref[:, pl.ds(BRANCH_W, BRANCH_W)]
    hs_ref[0] = a * jax.nn.sigmoid(gt)
    shifted_rows = tm + 2 * HALO - SUBLANES
    for b in range(1, SUBLANES):
        hs_ref[b, pl.ds(0, shifted_rows), :] = hs_ref[0, pl.ds(b, shifted_rows), :]

    def block(rb, carry):
        r0 = rb * CONF_ROWS
        grp = (CONF_ROWS // SUBLANES, SUBLANES, BRANCH_W)
        acc = jnp.zeros(grp, F32)
        for j in range(CF_WIDTH):
            a8, b = divmod(HALO - CF_WIDTH // 2 + j, SUBLANES)
            start = pl.multiple_of(r0 + SUBLANES * a8, SUBLANES)
            acc = acc + w_ref[j] * hs_ref[b, pl.ds(start, CONF_ROWS), :].reshape(grp)
        o_ref[pl.ds(pl.multiple_of(r0, CONF_ROWS), CONF_ROWS), :] = acc.reshape(CONF_ROWS, BRANCH_W) + b_ref[...]
        return carry

    lax.fori_loop(0, tm // CONF_ROWS, block, 0, unroll=2)
    acc = o_ref[...]
    mu = jnp.mean(acc, axis=-1, keepdims=True)
    cen = acc - mu
    var = jnp.mean(cen * cen, axis=-1, keepdims=True)
    y = cen * lax.rsqrt(var + EPS) * g_ref[...] + beta_ref[...]
    o_ref[...] = y * jax.nn.sigmoid(y)


def _conformer(u, dw_w, dw_b, ln_g, ln_b, L):
    T, C2 = u.shape
    C = C2 // 2
    tm = TOK_TILE
    vec = pl.BlockSpec((1, C), lambda i: (0, 0))
    return pl.pallas_call(
        functools.partial(_conf_kernel, tm=tm, L=L),
        grid=(T // tm,),
        in_specs=_halo_specs(tm, C2, T) + [pl.BlockSpec((CF_WIDTH, SUBLANES, C), lambda i: (0, 0, 0)), vec, vec, vec],
        out_specs=pl.BlockSpec((tm, C), lambda i: (i, 0)),
        out_shape=jax.ShapeDtypeStruct((T, C), F32),
        scratch_shapes=[pltpu.VMEM((tm + 2 * HALO, C2), F32), pltpu.VMEM((SUBLANES, tm + 2 * HALO, C), F32)],
        compiler_params=_cparams(("parallel",), 40),
        name="conformer_conv",
    )(u, u, u, dw_w, dw_b, ln_g, ln_b)


def _rel_bucket_np(rel):
    nb = REL_BUCKETS // 2
    max_exact = nb // 2
    ret = np.where(rel > 0, nb, 0)
    n = np.abs(rel)
    ratio = np.log(np.maximum(n, 1).astype(np.float32) / np.float32(max_exact)) / np.float32(
        math.log(REL_MAX_DIST / max_exact))
    large = max_exact + (ratio * np.float32(nb - max_exact)).astype(np.int32)
    large = np.minimum(large, nb - 1)
    return ret + np.where(n < max_exact, n, large)


def _bias_buckets():
    r = np.arange(ATT_TK)[:, None]
    c = np.arange(ATT_TQ)[None, :]
    out = np.zeros((2, 5, ATT_TK, ATT_TQ), np.int32)
    for e in range(2):
        for dj in range(-2, 3):
            out[e, dj + 2] = _rel_bucket_np(ATT_TK * dj + r - ATT_TQ * e - c)
    assert (out[:, 0] == REL_BUCKETS // 2 - 1).all() and (out[:, 4] == REL_BUCKETS - 1).all()
    return out


def _attn_kernel(q_ref, k_ref, vt_ref, bias_ref, lq1_ref, lk1_ref, lq2_ref, lk2_ref, sg_ref,
                 o_ref, qt_ref, s_ref, smax_ref, m_ref, acc_ref, *, nk, lam_init):
    qi = pl.program_id(2)
    qt = (q_ref[...] * jnp.asarray(DA_HEAD_DIM ** -0.5, BF16)).astype(F32).T
    feat = lax.broadcasted_iota(jnp.int32, qt.shape, 0)
    qt_ref[0] = jnp.where(feat < DA_HEAD_DIM, qt, 0.0).astype(BF16)
    qt_ref[1] = jnp.where(feat >= DA_HEAD_DIM, qt, 0.0).astype(BF16)
    m_ref[...] = jnp.full(m_ref.shape, NEG_BIG, F32)
    acc_ref[...] = jnp.zeros(acc_ref.shape, F32)
    j0 = qi // 2
    e = qi % 2

    def scores(j, slot):
        kblk = k_ref[pl.ds(pl.multiple_of(j * ATT_TK, ATT_TK), ATT_TK), :]
        kind = jnp.clip(j - j0, -2, 2) + 2
        for mm in range(2):
            s = jnp.dot(kblk, qt_ref[mm], preferred_element_type=F32) + bias_ref[e, kind]
            s_ref[slot, mm] = s
            smax_ref[slot, mm] = jnp.max(s, axis=0, keepdims=True)

    def softmax_values(j, slot):
        vblk = vt_ref[j]
        for mm in range(2):
            m_prev = m_ref[mm]
            m_new = jnp.maximum(m_prev, smax_ref[slot, mm])
            p = jnp.exp(s_ref[slot, mm] - m_new).astype(BF16)
            acc_ref[mm] = (jnp.exp(m_prev - m_new) * acc_ref[mm]
                           + jnp.dot(vblk, p, preferred_element_type=F32))
            m_ref[mm] = m_new

    scores(0, 0)

    def body(i, carry):
        for u in range(ATT_UNROLL):
            j = ATT_UNROLL * i + u
            cur = u % 2
            scores(jnp.minimum(j + 1, nk - 1), 1 - cur)
            softmax_values(j, cur)
        return carry

    lax.fori_loop(0, nk // ATT_UNROLL, body, 0)

    lam = (jnp.exp(jnp.sum(lq1_ref[...] * lk1_ref[...], keepdims=True))
           - jnp.exp(jnp.sum(lq2_ref[...] * lk2_ref[...], keepdims=True)) + lam_init)
    a0 = acc_ref[0]
    a1 = acc_ref[1]
    o = (a0[:DA_V_DIM] / a0[DA_V_DIM:DA_V_DIM + 1]
         - lam * (a1[:DA_V_DIM] / a1[DA_V_DIM:DA_V_DIM + 1]))
    ms = jnp.mean(o * o, axis=0, keepdims=True)
    y = o * lax.rsqrt(ms + EPS) * sg_ref[...] * (1.0 - lam_init)
    o_ref[...] = y.T


def _attention(qk, vt, bias, lq1, lk1, lq2, lk2, sg, lam_init):
    B, L, _ = qk.shape
    nk = L // ATT_TK
    assert nk % ATT_UNROLL == 0 and ATT_UNROLL % 2 == 0
    vec = pl.BlockSpec((1, DA_HEAD_DIM), lambda b, h, i: (0, 0))
    return pl.pallas_call(
        functools.partial(_attn_kernel, nk=nk, lam_init=lam_init),
        grid=(B, DA_HEADS, L // ATT_TQ),
        in_specs=[pl.BlockSpec((None, ATT_TQ, LANES), lambda b, h, i: (b, i, h)),
                  pl.BlockSpec((None, L, LANES), lambda b, h, i: (b, 0, DA_HEADS + h)),
                  pl.BlockSpec((None, None, nk, V_ROWS, ATT_TK), lambda b, h, i: (b, h, 0, 0, 0)),
                  pl.BlockSpec((None, 2, 5, ATT_TK, ATT_TQ), lambda b, h, i: (h, 0, 0, 0, 0)),
                  vec, vec, vec, vec,
                  pl.BlockSpec((DA_V_DIM, 1), lambda b, h, i: (0, 0))],
        out_specs=pl.BlockSpec((None, ATT_TQ, LANES), lambda b, h, i: (b, i, h)),
        out_shape=jax.ShapeDtypeStruct((B, L, BRANCH_W), F32),
        scratch_shapes=[pltpu.VMEM((2, LANES, ATT_TQ), BF16),
                        pltpu.VMEM((2, 2, ATT_TK, ATT_TQ), F32),
                        pltpu.VMEM((2, 2, 1, ATT_TQ), F32),
                        pltpu.VMEM((2, 1, ATT_TQ), F32),
                        pltpu.VMEM((2, V_ROWS, ATT_TQ), F32)],
        compiler_params=_cparams(("parallel", "parallel", "arbitrary"), 52),
        name="diff_attention",
    )(qk, qk, vt, bias, lq1, lk1, lq2, lk2, sg)


def _fft_consts(n2):
    n1 = FFT_N1
    n = n1 * n2
    f2 = np.exp(-2j * np.pi * np.outer(np.arange(n2), np.arange(n2)) / n2)
    tw = np.exp(-2j * np.pi * np.outer(np.arange(n2), np.arange(n1)) / n)
    f1 = np.exp(-2j * np.pi * np.outer(np.arange(n1), np.arange(n1)) / n1)
    c = {}
    c["f2_full"] = np.concatenate([f2.real, f2.imag], 0)
    c["f2_half"] = c["f2_full"][:, : n2 // 2]
    c["tre"], c["tim"] = tw.real, tw.imag
    c["g_top"] = np.concatenate([f1.real, f1.imag], 1)
    c["g_bot"] = np.concatenate([-f1.imag, f1.real], 1)
    c["gi_top"] = np.concatenate([f1.real, -f1.imag], 1)
    c["gi_bot"] = np.concatenate([f1.imag, f1.real], 1)
    c["fin_re"] = f2.real.T[: n2 // 2]
    c["fin_im"] = f2.imag.T[: n2 // 2]
    out = {}
    for k, v in c.items():
        out[k] = jnp.asarray(v, F32 if k in ("tre", "tim") else BF16)
    return out


def _filter_td_kernel(emb_ref, t_ref, w1_ref, b1_ref, f0_ref, w2_ref, b2_ref, f1_ref, w3_ref, ad_ref, o_ref):
    emb = emb_ref[...].astype(BF16)
    h = jnp.dot(w1_ref[...], emb, preferred_element_type=F32) + b1_ref[...]
    h = jnp.sin(f0_ref[...] * h)
    h = jnp.dot(w2_ref[...], h.astype(BF16), preferred_element_type=F32) + b2_ref[...]
    h = jnp.sin(f1_ref[...] * h)
    h = jnp.dot(w3_ref[...], h.astype(BF16), preferred_element_type=F32)
    decay = jnp.exp(-(ad_ref[...] * t_ref[...]))
    o_ref[...] = h * decay


def _filter_td(emb_t, t_full, w1t, b1, f0, w2t, b2, f1, w3t, absdelta):
    n = emb_t.shape[1]
    pt = 2048
    nch = n // pt
    col = lambda r: pl.BlockSpec((r, 1), lambda i: (0, 0))
    return pl.pallas_call(
        _filter_td_kernel,
        grid=(nch,),
        in_specs=[pl.BlockSpec((HY_EMB_PAD, pt), lambda i: (0, i)),
                  pl.BlockSpec((1, pt), lambda i: (0, i)),
                  pl.BlockSpec((HY_ORDER, HY_EMB_PAD), lambda i: (0, 0)), col(HY_ORDER), col(HY_ORDER),
                  pl.BlockSpec((HY_ORDER, HY_ORDER), lambda i: (0, 0)), col(HY_ORDER), col(HY_ORDER),
                  pl.BlockSpec((None, BRANCH_W, HY_ORDER), lambda i: (i // (nch // 2), 0, 0)),
                  col(BRANCH_W)],
        out_specs=pl.BlockSpec((BRANCH_W, pt), lambda i: (0, i)),
        out_shape=jax.ShapeDtypeStruct((BRANCH_W, n), F32),
        compiler_params=_cparams(("parallel",), 32),
        name="hyena_filter_time",
    )(emb_t, t_full, w1t, b1, f0, w2t, b2, f1, w3t, absdelta)


def _cmul(are, aim, bre, bim):
    return are * bre - aim * bim, are * bim + aim * bre


def _filter_fft_kernel(f_ref, f2_ref, tre_ref, tim_ref, gt_ref, gb_ref, o_ref, bre_ref, bim_ref, *, cc, n2):
    n_total = FFT_N1 * n2
    row = lax.broadcasted_iota(jnp.int32, (n2, FFT_N1), 0)
    lane = lax.broadcasted_iota(jnp.int32, (n2, FFT_N1), 1)
    is_lag0_backward = (row == n2 // 2) & (lane == 0)

    def per_channel(c, carry):
        f = f_ref[c]
        s = jnp.sum(jnp.abs(f), keepdims=True)
        f = jnp.where(is_lag0_backward, 0.0, f) * (1.0 / ((s + EPS) * n_total))
        a = jnp.dot(f2_ref[...], f.astype(BF16), preferred_element_type=F32)
        bre, bim = _cmul(a[:n2], a[n2:], tre_ref[...], tim_ref[...])
        bre_ref[c] = bre.astype(BF16)
        bim_ref[c] = bim.astype(BF16)
        return carry

    lax.fori_loop(0, cc, per_channel, 0, unroll=HY_CH_UNROLL)
    bre = bre_ref[...].reshape(cc * n2, FFT_N1)
    bim = bim_ref[...].reshape(cc * n2, FFT_N1)
    o_ref[...] = (jnp.dot(bre, gt_ref[...], preferred_element_type=F32)
                  + jnp.dot(bim, gb_ref[...], preferred_element_type=F32))


def _filter_fft(filt, consts, n2, cc):
    C = filt.shape[0]
    full = lambda a: pl.BlockSpec(a.shape, lambda i: (0,) * a.ndim)
    mats = [consts["f2_full"], consts["tre"], consts["tim"], consts["g_top"], consts["g_bot"]]
    return pl.pallas_call(
        functools.partial(_filter_fft_kernel, cc=cc, n2=n2),
        grid=(C // cc,),
        in_specs=[pl.BlockSpec((cc, n2, FFT_N1), lambda i: (i, 0, 0))] + [full(a) for a in mats],
        out_specs=pl.BlockSpec((cc * n2, 2 * FFT_N1), lambda i: (i, 0)),
        out_shape=jax.ShapeDtypeStruct((C * n2, 2 * FFT_N1), F32),
        scratch_shapes=[pltpu.VMEM((cc, n2, FFT_N1), BF16), pltpu.VMEM((cc, n2, FFT_N1), BF16)],
        compiler_params=_cparams(("parallel",), 40),
        name="hyena_filter_fft",
    )(filt, *mats)


def _shift_prev(x, row, lane):
    a = pltpu.roll(x, 1, axis=1)
    b = pltpu.roll(a, 1, axis=0)
    return jnp.where(lane == 0, jnp.where(row == 0, 0.0, b), a)


def _shift_next(x, row, lane, rows):
    a = pltpu.roll(x, FFT_N1 - 1, axis=1)
    b = pltpu.roll(a, rows - 1, axis=0)
    return jnp.where(lane == FFT_N1 - 1, jnp.where(row == rows - 1, 0.0, b), a)


def _hyena_kernel(sw_ref, sb_ref, d_ref, x0_ref, x1_ref, v_ref, hf_ref, f2_ref, tre_ref, tim_ref,
                  gt_ref, gb_ref, git_ref, gib_ref, fre_ref, fim_ref, o_ref,
                  z_ref, x0c_ref, bre_ref, bim_ref, c_ref, *, cc, n2):
    rows = n2 // 2
    ch0 = pl.program_id(1) * cc
    row = lax.broadcasted_iota(jnp.int32, (rows, FFT_N1), 0)
    lane = lax.broadcasted_iota(jnp.int32, (rows, FFT_N1), 1)

    def short_conv(x, ch):
        return (sw_ref[0, ch] * _shift_prev(x, row, lane) + sw_ref[1, ch] * x
                + sw_ref[2, ch] * _shift_next(x, row, lane, rows) + sb_ref[ch])

    def forward(c, carry):
        ch = ch0 + c
        x0c_ref[c] = short_conv(x0_ref[c], ch)
        z = short_conv(x1_ref[c], BRANCH_W + ch) * short_conv(v_ref[c], 2 * BRANCH_W + ch)
        z_ref[c] = z
        a = jnp.dot(f2_ref[...], z.astype(BF16), preferred_element_type=F32)
        bre, bim = _cmul(a[:n2], a[n2:], tre_ref[...], tim_ref[...])
        bre_ref[c] = bre.astype(BF16)
        bim_ref[c] = bim.astype(BF16)
        return carry

    lax.fori_loop(0, cc, forward, 0, unroll=HY_CH_UNROLL)

    bre = bre_ref[...].reshape(cc * n2, FFT_N1)
    bim = bim_ref[...].reshape(cc * n2, FFT_N1)
    x = (jnp.dot(bre, gt_ref[...], preferred_element_type=F32)
         + jnp.dot(bim, gb_ref[...], preferred_element_type=F32))
    hf = hf_ref[...]
    yre, yim = _cmul(x[:, :FFT_N1], x[:, FFT_N1:], hf[:, :FFT_N1], hf[:, FFT_N1:])
    cfull = (jnp.dot(yre.astype(BF16), git_ref[...], preferred_element_type=F32)
             + jnp.dot(yim.astype(BF16), gib_ref[...], preferred_element_type=F32))
    c_ref[...] = cfull.reshape(cc, n2, 2 * FFT_N1)

    def backward(c, carry):
        cm = c_ref[c]
        cre, cim = cm[:, :FFT_N1], cm[:, FFT_N1:]
        tre, tim = tre_ref[...], tim_ref[...]
        dre = cre * tre + cim * tim
        dim = cim * tre - cre * tim
        y = (jnp.dot(fre_ref[...], dre.astype(BF16), preferred_element_type=F32)
             + jnp.dot(fim_ref[...], dim.astype(BF16), preferred_element_type=F32))
        o_ref[c] = x0c_ref[c] * (y + z_ref[c] * d_ref[ch0 + c])
        return carry

    lax.fori_loop(0, cc, backward, 0, unroll=HY_CH_UNROLL)


def _hyena(u, hf, consts, short_w, short_b, d_bias, n2, cc):
    B = u.shape[0]
    rows = n2 // 2
    C = BRANCH_W
    nblk = C // cc
    smem = pl.BlockSpec(memory_space=pltpu.SMEM)
    full = lambda a: pl.BlockSpec(a.shape, lambda b, i: (0,) * a.ndim)
    mats = [consts[k] for k in ("f2_half", "tre", "tim", "g_top", "g_bot", "gi_top", "gi_bot", "fin_re", "fin_im")]
    sec = lambda s: pl.BlockSpec((None, cc, rows, FFT_N1), lambda b, i: (b, s * nblk + i, 0, 0))
    return pl.pallas_call(
        functools.partial(_hyena_kernel, cc=cc, n2=n2),
        grid=(B, nblk),
        in_specs=[smem, smem, smem, sec(0), sec(1), sec(2),
                  pl.BlockSpec((cc * n2, 2 * FFT_N1), lambda b, i: (i, 0))] + [full(a) for a in mats],
        out_specs=pl.BlockSpec((None, cc, rows, FFT_N1), lambda b, i: (b, i, 0, 0)),
        out_shape=jax.ShapeDtypeStruct((B, C, rows, FFT_N1), F32),
        scratch_shapes=[pltpu.VMEM((cc, rows, FFT_N1), F32), pltpu.VMEM((cc, rows, FFT_N1), F32),
                        pltpu.VMEM((cc, n2, FFT_N1), BF16), pltpu.VMEM((cc, n2, FFT_N1), BF16),
                        pltpu.VMEM((cc, n2, 2 * FFT_N1), F32)],
        compiler_params=_cparams(("parallel", "arbitrary"), 40),
        name="hyena_mixer",
    )(short_w, short_b, d_bias, u, u, u, hf, *mats)


def _hyena_tables(L):
    t = jnp.linspace(0.0, 1.0, L, dtype=F32)[:, None]
    w = 2.0 * math.pi * jnp.arange(L, dtype=F32)[:, None] / L
    bands = jnp.linspace(1e-4, HY_BANDS - 1, HY_BANDS, dtype=F32)[None, :]
    emb = jnp.concatenate([t, jnp.cos(bands * w), -jnp.sin(bands * w)], axis=-1)
    two_sided = lambda a: jnp.concatenate([a, a[:1], a[:0:-1]], axis=0)
    emb_full = jnp.pad(two_sided(emb), ((0, 0), (0, HY_EMB_PAD - HY_EMB)))
    t_full = two_sided(t).T
    return emb_full.T, t_full


def _merge_kernel(x_ref, ya_ref, yb_ref, yct_ref, yd_ref, g_ref, ws_ref, wm_ref, wb_ref, wo_ref, fg_ref, o_ref,
                  *, final):
    x = x_ref[...]
    ms = jnp.mean(x * x, axis=-1, keepdims=True)
    h = (x * lax.rsqrt(ms + EPS) * g_ref[...]).astype(BF16)
    mixed = None
    for n in range(N_BRANCH):
        if n == 0:
            y = ya_ref[...]
        elif n == 1:
            y = yb_ref[...]
        elif n == 2:
            y = yct_ref[...].T
        else:
            y = yd_ref[...]
        sg = jnp.dot(h, ws_ref[n], preferred_element_type=F32)
        u = (y * (sg * jax.nn.sigmoid(sg))).astype(BF16)
        proj = jnp.dot(u, wb_ref[n], preferred_element_type=F32)
        merge = jax.nn.sigmoid(jnp.dot(h, wm_ref[n], preferred_element_type=F32))
        mixed = merge * proj if mixed is None else mixed + merge * proj
    out = x + jnp.dot(mixed.astype(BF16), wo_ref[...], preferred_element_type=F32)
    if final:
        ms = jnp.mean(out * out, axis=-1, keepdims=True)
        out = out * lax.rsqrt(ms + EPS) * fg_ref[...]
    o_ref[...] = out


def _merge(x2, ya, yb, yct, yd, g, ws, wm, wb, wo, fg, L, final):
    T, D = x2.shape
    tm = 256
    nl = L // tm
    W = BRANCH_W
    tok = lambda c: pl.BlockSpec((tm, c), lambda i: (i, 0))
    const = lambda a: pl.BlockSpec(a.shape, lambda i: (0,) * a.ndim, pipeline_mode=pl.Buffered(1))
    return pl.pallas_call(
        functools.partial(_merge_kernel, final=final),
        grid=(T // tm,),
        in_specs=[tok(D), tok(W), tok(W),
                  pl.BlockSpec((None, W, tm), lambda i: (i // nl, 0, i % nl)),
                  tok(W), const(g), const(ws), const(wm), const(wb), const(wo), const(fg)],
        out_specs=tok(D),
        out_shape=jax.ShapeDtypeStruct((T, D), F32),
        compiler_params=_cparams(("parallel",), 52),
        name="gated_merge",
    )(x2, ya, yb, yct, yd, g, ws, wm, wb, wo, fg)


def _layer_weights(l, norm_g, w_in, pool_w, pool_scale, subln_g, rel_bias, hy_short_w, hy_short_b, hy_w1, hy_b1,
                   hy_freq, hy_w2, hy_b2, hy_w3, hy_d, cf_dw_w, cf_dw_b, cf_ln_g, cf_ln_b, w_branch, w_out):
    w = w_in[l].astype(BF16)
    W = BRANCH_W
    p = {}
    p["g"] = norm_g[l][None, :]
    p["w_tok"] = jnp.concatenate([w[:, OFF_POOL:OFF_V], w[:, OFF_CF:OFF_SILU]], axis=1)
    p["w_chan"] = w[:, OFF_V:OFF_CF].T
    p["w_silu"] = w[:, OFF_SILU:OFF_MERGE].reshape(D_MODEL, N_BRANCH, W).transpose(1, 0, 2)
    p["w_merge"] = w[:, OFF_MERGE:].reshape(D_MODEL, N_BRANCH, D_MODEL).transpose(1, 0, 2)
    p["w_branch"] = w_branch[l].astype(BF16)
    p["w_out"] = w_out[l].astype(BF16)
    p["pool_w"] = pool_w[l].astype(BF16)
    p["pool_scale"] = pool_scale[l][None, :]
    p["subln_g"] = subln_g[l][:, None]
    buckets = jnp.asarray(_bias_buckets())[None]
    table_t = rel_bias.astype(F32).T
    bias = jnp.zeros((DA_HEADS,) + buckets.shape[1:], F32)
    for b in range(REL_BUCKETS):
        bias = jnp.where(buckets == b, table_t[:, b][:, None, None, None, None], bias)
    p["bias"] = bias
    p["short_w"] = hy_short_w[l]
    p["short_b"] = hy_short_b[l]
    p["hy_d"] = hy_d[l]
    p["w1t"] = jnp.pad(hy_w1[l].T, ((0, 0), (0, HY_EMB_PAD - HY_EMB))).astype(BF16)
    p["b1"] = hy_b1[l][:, None]
    p["f0"] = hy_freq[l, 0][:, None]
    p["w2t"] = hy_w2[l].T.astype(BF16)
    p["b2"] = hy_b2[l][:, None]
    p["f1"] = hy_freq[l, 1][:, None]
    p["w3t"] = hy_w3[l].T.reshape(2, W, HY_ORDER).astype(BF16)
    max_decay = math.log(HY_DECAY_TARGET) / HY_FAST
    min_decay = math.log(HY_DECAY_TARGET) / HY_SLOW
    p["absdelta"] = jnp.abs(jnp.linspace(min_decay, max_decay, W, dtype=F32))[:, None]
    p["dw_w"] = jnp.broadcast_to(cf_dw_w[l][:, None, :], (CF_WIDTH, SUBLANES, BRANCH_W))
    p["dw_b"] = cf_dw_b[l][None, :]
    p["ln_g"] = cf_ln_g[l][None, :]
    p["ln_b"] = cf_ln_b[l][None, :]
    return p


def _hyena_chunk(n2):
    return max(1, 512 // n2)


def _filter_spectrum(p, L):
    n2 = 2 * L // FFT_N1
    consts = _fft_consts(n2)
    emb_t, t_full = _hyena_tables(L)
    filt = _filter_td(emb_t, t_full, p["w1t"], p["b1"], p["f0"], p["w2t"], p["b2"], p["f1"], p["w3t"],
                      p["absdelta"])
    hf = _filter_fft(filt.reshape(BRANCH_W, n2, FFT_N1), consts, n2, _hyena_chunk(n2))
    return hf, consts, n2


def _layer(x, p, l, lq1, lk1, lq2, lk2, final_g, final):
    B, L, D = x.shape
    T = B * L
    x2 = x.reshape(T, D)
    g = p["g"]
    lam_init = 0.8 - 0.6 * math.exp(-0.3 * l)

    pool_in, qk, cf, vt, hyt = _project(x2, g, p["w_tok"], p["w_chan"], B, L)
    qk = qk.reshape(B, L, 2 * BRANCH_W)

    ya = _pool(pool_in, p["pool_w"], p["pool_scale"], L)
    yb = _attention(qk, vt, p["bias"], lq1[l][None, :], lk1[l][None, :], lq2[l][None, :],
                    lk2[l][None, :], p["subln_g"], lam_init).reshape(T, BRANCH_W)
    hf, consts, n2 = _filter_spectrum(p, L)
    yct = _hyena(hyt.reshape(B, 3 * BRANCH_W, n2 // 2, FFT_N1), hf, consts, p["short_w"], p["short_b"], p["hy_d"],
                 n2, _hyena_chunk(n2)).reshape(B, BRANCH_W, L)
    yd = _conformer(cf, p["dw_w"], p["dw_b"], p["ln_g"], p["ln_b"], L)

    out = _merge(x2, ya, yb, yct, yd, g, p["w_silu"], p["w_merge"], p["w_branch"], p["w_out"], final_g[None, :], L,
                 final)
    return out.reshape(B, L, D)


def kernel(x_prompt, x_sample, norm_g, w_in, pool_w, pool_scale, lambda_q1, lambda_k1, lambda_q2, lambda_k2, subln_g, rel_bias, hy_short_w, hy_short_b, hy_w1, hy_b1, hy_freq, hy_w2, hy_b2, hy_w3, hy_d, cf_dw_w, cf_dw_b, cf_ln_g, cf_ln_b, w_branch, w_out, final_g):
    depth = norm_g.shape[0]
    hp, hs = x_prompt, x_sample
    for l in range(depth):
        p = _layer_weights(l, norm_g, w_in, pool_w, pool_scale, subln_g, rel_bias, hy_short_w, hy_short_b, hy_w1,
                           hy_b1, hy_freq, hy_w2, hy_b2, hy_w3, hy_d, cf_dw_w, cf_dw_b, cf_ln_g, cf_ln_b, w_branch,
                           w_out)
        final = l == depth - 1
        hp = _layer(hp, p, l, lambda_q1, lambda_k1, lambda_q2, lambda_k2, final_g, final)
        hs = _layer(hs, p, l, lambda_q1, lambda_k1, lambda_q2, lambda_k2, final_g, final)
    return (hp, hs)
```

```python
import functools
import math

import numpy as np
import jax
import jax.numpy as jnp
from jax import lax
from jax.experimental import pallas as pl
from jax.experimental.pallas import tpu as pltpu

F32 = jnp.float32
BF16 = jnp.bfloat16

D_MODEL = 1024
BRANCH_W = 512
N_BRANCH = 4
POOL_WINDOWS = (2, 4, 8, 16)
POOL_GW = 128
DA_HEADS = 4
DA_HEAD_DIM = 64
DA_V_DIM = 128
REL_BUCKETS = 32
REL_MAX_DIST = 128
HY_EMB = 33
HY_EMB_PAD = 40
HY_BANDS = 16
HY_ORDER = 64
HY_DECAY_TARGET = 1e-2
HY_FAST = 0.3
HY_SLOW = 1.5
CF_WIDTH = 31
EPS = 1e-6

OFF_POOL = 0
OFF_Q = 512
OFF_K = 1024
OFF_V = 1536
OFF_HY = 2048
OFF_CF = 3584
OFF_SILU = 4608
OFF_MERGE = 6656

LANES = 128
SUBLANES = 8
CONF_ROWS = 32
FFT_N1 = 256
HALO = 16
TOK_TILE = 512
ATT_TQ = 256
ATT_TK = 512
ATT_UNROLL = 4
ATT_NEAR = 4
HY_CH_UNROLL = 4
NEG_BIG = -1e30
LOG2E = math.log2(math.e)
V_ROWS = DA_V_DIM + 16


def _cparams(sem, vmem_mb, flags=None):
    return pltpu.CompilerParams(dimension_semantics=sem, vmem_limit_bytes=vmem_mb << 20, flags=flags)


def _proj_kernel(x_ref, g_ref, wtok_ref, wchan_ref, pool_ref, qk_ref, cf_ref, vt_ref, hyt_ref):
    x = x_ref[...]
    ms = jnp.mean(x * x, axis=-1, keepdims=True)
    h = (x * lax.rsqrt(ms + EPS) * g_ref[...]).astype(BF16)
    W = BRANCH_W

    def tok(lo, hi):
        return jnp.dot(h, wtok_ref[:, lo:hi], preferred_element_type=F32)

    def chan(lo, hi):
        return lax.dot_general(wchan_ref[lo:hi, :], h, (((1,), (1,)), ((), ())), preferred_element_type=F32)

    pool_ref[...] = tok(0, W)
    qk_ref[:, pl.ds(0, W)] = (tok(W, 2 * W) * (LOG2E * DA_HEAD_DIM ** -0.5)).astype(BF16)
    qk_ref[:, pl.ds(W, W)] = tok(2 * W, 3 * W).astype(BF16)
    cf_ref[...] = tok(3 * W, 5 * W)
    for hd in range(DA_HEADS):
        vt_ref[hd, 0, pl.ds(0, DA_V_DIM), :] = chan(hd * DA_V_DIM, (hd + 1) * DA_V_DIM).astype(BF16)
        vt_ref[hd, 0, pl.ds(DA_V_DIM, V_ROWS - DA_V_DIM), :] = jnp.ones((V_ROWS - DA_V_DIM, x.shape[0]), BF16)
    for s in range(3):
        hyt_ref[pl.ds(s * W, W), :] = chan((1 + s) * W, (2 + s) * W)


def _project(x2, g, wtok, wchan, B, L):
    T, D = x2.shape
    tm = ATT_TK
    nl = L // tm
    W = BRANCH_W
    const = lambda a: pl.BlockSpec(a.shape, lambda i: (0,) * a.ndim, pipeline_mode=pl.Buffered(1))
    tok = lambda c: pl.BlockSpec((tm, c), lambda i: (i, 0))
    return pl.pallas_call(
        _proj_kernel,
        grid=(T // tm,),
        in_specs=[tok(D), const(g), const(wtok), const(wchan)],
        out_specs=[tok(W), tok(2 * W), tok(2 * W),
                   pl.BlockSpec((None, DA_HEADS, 1, V_ROWS, tm), lambda i: (i // nl, 0, i % nl, 0, 0)),
                   pl.BlockSpec((None, 3 * W, tm), lambda i: (i // nl, 0, i % nl))],
        out_shape=[jax.ShapeDtypeStruct((T, W), F32), jax.ShapeDtypeStruct((T, 2 * W), BF16),
                   jax.ShapeDtypeStruct((T, 2 * W), F32),
                   jax.ShapeDtypeStruct((B, DA_HEADS, nl, V_ROWS, tm), BF16),
                   jax.ShapeDtypeStruct((B, 3 * W, L), F32)],
        compiler_params=_cparams(("parallel",), 48),
        name="input_projection",
    )(x2, g, wtok, wchan)


def _halo_specs(tm, C, T):
    r = tm // HALO
    last = T // HALO - 1
    return [pl.BlockSpec((tm, C), lambda i: (i, 0)),
            pl.BlockSpec((HALO, C), lambda i: (jnp.maximum(i * r - 1, 0), 0)),
            pl.BlockSpec((HALO, C), lambda i: (jnp.minimum((i + 1) * r, last), 0))]


def _fill_ext(ext_ref, cur_ref, prev_ref, next_ref, tm, L):
    i = pl.program_id(0)
    tiles_per_seq = L // tm
    first = (i % tiles_per_seq) == 0
    lastt = (i % tiles_per_seq) == tiles_per_seq - 1
    ext_ref[pl.ds(0, HALO), :] = jnp.where(first, 0.0, prev_ref[...])
    ext_ref[pl.ds(HALO, tm), :] = cur_ref[...]
    ext_ref[pl.ds(HALO + tm, HALO), :] = jnp.where(lastt, 0.0, next_ref[...])


def _pool_kernel(cur_ref, prev_ref, next_ref, w_ref, sc_ref, o_ref, ext_ref, *, tm, L):
    _fill_ext(ext_ref, cur_ref, prev_ref, next_ref, tm, L)
    i = pl.program_id(0)
    pos = (i % (L // tm)) * tm + lax.broadcasted_iota(jnp.int32, (tm, 1), 0)
    for g, w in enumerate(POOL_WINDOWS):
        lanes = pl.ds(g * POOL_GW, POOL_GW)
        s = None
        for d in range(-(w // 2), w - w // 2):
            t = ext_ref[pl.ds(HALO + d, tm), lanes]
            s = t if s is None else s + t
        lo = jnp.maximum(pos - w // 2, 0)
        hi = jnp.minimum(pos + (w - 1 - w // 2), L - 1)
        cnt = (hi - lo + 1).astype(F32)
        pooled = s / cnt - ext_ref[pl.ds(HALO, tm), lanes]
        mixed = jnp.dot(pooled.astype(BF16), w_ref[g], preferred_element_type=F32)
        o_ref[:, lanes] = mixed * sc_ref[:, lanes]


def _pool(u, w_mix, scale, L):
    T, C = u.shape
    tm = TOK_TILE
    return pl.pallas_call(
        functools.partial(_pool_kernel, tm=tm, L=L),
        grid=(T // tm,),
        in_specs=_halo_specs(tm, C, T) + [
            pl.BlockSpec((4, POOL_GW, POOL_GW), lambda i: (0, 0, 0)),
            pl.BlockSpec((1, C), lambda i: (0, 0))],
        out_specs=pl.BlockSpec((tm, C), lambda i: (i, 0)),
        out_shape=jax.ShapeDtypeStruct((T, C), F32),
        scratch_shapes=[pltpu.VMEM((tm + 2 * HALO, C), F32)],
        compiler_params=_cparams(("parallel",), 32),
        name="pool_mixer",
    )(u, u, u, w_mix, scale)


def _conf_kernel(cur_ref, prev_ref, next_ref, w_ref, b_ref, g_ref, beta_ref, o_ref, ext_ref, hs_ref, *, tm, L):
    _fill_ext(ext_ref, cur_ref, prev_ref, next_ref, tm, L)
    a = ext_ref[:, pl.ds(0, BRANCH_W)]
    gt = ext_ref[:, pl.ds(BRANCH_W, BRANCH_W)]
    hs_ref[0] = a * jax.nn.sigmoid(gt)
    shifted_rows = tm + 2 * HALO - SUBLANES
    for b in range(1, SUBLANES):
        hs_ref[b, pl.ds(0, shifted_rows), :] = hs_ref[0, pl.ds(b, shifted_rows), :]

    def block(rb, carry):
        r0 = rb * CONF_ROWS
        grp = (CONF_ROWS // SUBLANES, SUBLANES, BRANCH_W)
        acc = jnp.zeros(grp, F32)
        for j in range(CF_WIDTH):
            a8, b = divmod(HALO - CF_WIDTH // 2 + j, SUBLANES)
            start = pl.multiple_of(r0 + SUBLANES * a8, SUBLANES)
            acc = acc + w_ref[j] * hs_ref[b, pl.ds(start, CONF_ROWS), :].reshape(grp)
        o_ref[pl.ds(pl.multiple_of(r0, CONF_ROWS), CONF_ROWS), :] = acc.reshape(CONF_ROWS, BRANCH_W) + b_ref[...]
        return carry

    lax.fori_loop(0, tm // CONF_ROWS, block, 0, unroll=2)
    acc = o_ref[...]
    mu = jnp.mean(acc, axis=-1, keepdims=True)
    cen = acc - mu
    var = jnp.mean(cen * cen, axis=-1, keepdims=True)
    y = cen * lax.rsqrt(var + EPS) * g_ref[...] + beta_ref[...]
    o_ref[...] = y * jax.nn.sigmoid(y)


def _conformer(u, dw_w, dw_b, ln_g, ln_b, L):
    T, C2 = u.shape
    C = C2 // 2
    tm = TOK_TILE
    vec = pl.BlockSpec((1, C), lambda i: (0, 0))
    return pl.pallas_call(
        functools.partial(_conf_kernel, tm=tm, L=L),
        grid=(T // tm,),
        in_specs=_halo_specs(tm, C2, T) + [pl.BlockSpec((CF_WIDTH, SUBLANES, C), lambda i: (0, 0, 0)), vec, vec, vec],
        out_specs=pl.BlockSpec((tm, C), lambda i: (i, 0)),
        out_shape=jax.ShapeDtypeStruct((T, C), F32),
        scratch_shapes=[pltpu.VMEM((tm + 2 * HALO, C2), F32), pltpu.VMEM((SUBLANES, tm + 2 * HALO, C), F32)],
        compiler_params=_cparams(("parallel",), 40),
        name="conformer_conv",
    )(u, u, u, dw_w, dw_b, ln_g, ln_b)


def _rel_bucket_np(rel):
    nb = REL_BUCKETS // 2
    max_exact = nb // 2
    ret = np.where(rel > 0, nb, 0)
    n = np.abs(rel)
    ratio = np.log(np.maximum(n, 1).astype(np.float32) / np.float32(max_exact)) / np.float32(
        math.log(REL_MAX_DIST / max_exact))
    large = max_exact + (ratio * np.float32(nb - max_exact)).astype(np.int32)
    large = np.minimum(large, nb - 1)
    return ret + np.where(n < max_exact, n, large)


def _bias_buckets():
    r = np.arange(ATT_TK)[:, None]
    c = np.arange(ATT_TQ)[None, :]
    span = ATT_NEAR - 1
    out = np.zeros((2, 2 * span + 1, ATT_TK, ATT_TQ), np.int32)
    for e in range(2):
        for dj in range(-span, span + 1):
            out[e, dj + span] = _rel_bucket_np(ATT_TK * dj + r - ATT_TQ * e - c)
        assert (out[e, :span - 1] == REL_BUCKETS // 2 - 1).all() and (out[e, span + 2:] == REL_BUCKETS - 1).all()
    return out


def _attn_kernel(cfar_ref, q_ref, k_ref, vt_ref, bias_ref, lq1_ref, lk1_ref, lq2_ref, lk2_ref, sg_ref,
                 o_ref, qt_ref, s_ref, smax_ref, m_ref, acc_ref, *, nk, lam_init):
    h = pl.program_id(1)
    qi = pl.program_id(2)
    qt = q_ref[...].astype(F32).T
    feat = lax.broadcasted_iota(jnp.int32, qt.shape, 0)
    qt_ref[0] = jnp.where(feat < DA_HEAD_DIM, qt, 0.0).astype(BF16)
    qt_ref[1] = jnp.where(feat >= DA_HEAD_DIM, qt, 0.0).astype(BF16)
    m_ref[...] = jnp.full(m_ref.shape, NEG_BIG, F32)
    acc_ref[...] = jnp.zeros(acc_ref.shape, F32)
    j0 = qi // 2
    e = qi % 2
    c_left = cfar_ref[h, 0]
    c_right = cfar_ref[h, 1]

    def scores(j, slot, near_kind=None):
        kblk = k_ref[pl.ds(pl.multiple_of(j * ATT_TK, ATT_TK), ATT_TK), :]
        for mm in range(2):
            s = jnp.dot(kblk, qt_ref[mm], preferred_element_type=F32)
            if near_kind is not None:
                s = s + bias_ref[e, near_kind]
            s_ref[slot, mm] = s
            smax_ref[slot, mm] = jnp.max(s, axis=0, keepdims=True)

    def softmax_values(j, slot, off):
        vblk = vt_ref[j]
        for mm in range(2):
            m_prev = m_ref[mm]
            m_new = jnp.maximum(m_prev, smax_ref[slot, mm] + off)
            p = jnp.exp2(s_ref[slot, mm] - (m_new - off)).astype(BF16)
            acc_ref[mm] = (jnp.exp2(m_prev - m_new) * acc_ref[mm]
                           + jnp.dot(vblk, p, preferred_element_type=F32))
            m_ref[mm] = m_new

    near_lo = jnp.clip(j0 - 1, 0, nk - ATT_NEAR)
    n_far = nk - ATT_NEAR

    def far_chunk(t):
        t = jnp.minimum(t, n_far - 1)
        return jnp.where(t < near_lo, t, t + ATT_NEAR)

    def far_offset(j):
        return jnp.where(j < j0, c_left, c_right)

    scores(near_lo, 0, near_lo - j0 + ATT_NEAR - 1)
    for i in range(ATT_NEAR):
        j = near_lo + i
        if i + 1 < ATT_NEAR:
            scores(j + 1, (i + 1) % 2, j + 1 - j0 + ATT_NEAR - 1)
        else:
            scores(far_chunk(0), (i + 1) % 2)
        softmax_values(j, i % 2, 0.0)

    def body(i, carry):
        for u in range(ATT_UNROLL):
            t = ATT_UNROLL * i + u
            cur = u % 2
            scores(far_chunk(t + 1), 1 - cur)
            j = far_chunk(t)
            softmax_values(j, cur, far_offset(j))
        return carry

    lax.fori_loop(0, n_far // ATT_UNROLL, body, 0)

    lam = (jnp.exp(jnp.sum(lq1_ref[...] * lk1_ref[...], keepdims=True))
           - jnp.exp(jnp.sum(lq2_ref[...] * lk2_ref[...], keepdims=True)) + lam_init)
    a0 = acc_ref[0]
    a1 = acc_ref[1]
    o = (a0[:DA_V_DIM] / a0[DA_V_DIM:DA_V_DIM + 1]
         - lam * (a1[:DA_V_DIM] / a1[DA_V_DIM:DA_V_DIM + 1]))
    ms = jnp.mean(o * o, axis=0, keepdims=True)
    y = o * lax.rsqrt(ms + EPS) * sg_ref[...] * (1.0 - lam_init)
    o_ref[...] = y.T


def _attention(qk, vt, bias, cfar, lq1, lk1, lq2, lk2, sg, lam_init):
    B, L, _ = qk.shape
    nk = L // ATT_TK
    assert (nk - ATT_NEAR) % ATT_UNROLL == 0 and ATT_UNROLL % 2 == 0 and ATT_NEAR % 2 == 0 and nk > ATT_NEAR
    vec = pl.BlockSpec((1, DA_HEAD_DIM), lambda b, h, i: (0, 0))
    return pl.pallas_call(
        functools.partial(_attn_kernel, nk=nk, lam_init=lam_init),
        grid=(B, DA_HEADS, L // ATT_TQ),
        in_specs=[pl.BlockSpec(memory_space=pltpu.SMEM),
                  pl.BlockSpec((None, ATT_TQ, LANES), lambda b, h, i: (b, i, h)),
                  pl.BlockSpec((None, L, LANES), lambda b, h, i: (b, 0, DA_HEADS + h)),
                  pl.BlockSpec((None, None, nk, V_ROWS, ATT_TK), lambda b, h, i: (b, h, 0, 0, 0)),
                  pl.BlockSpec((None, 2, 2 * ATT_NEAR - 1, ATT_TK, ATT_TQ), lambda b, h, i: (h, 0, 0, 0, 0)),
                  vec, vec, vec, vec,
                  pl.BlockSpec((DA_V_DIM, 1), lambda b, h, i: (0, 0))],
        out_specs=pl.BlockSpec((None, ATT_TQ, LANES), lambda b, h, i: (b, i, h)),
        out_shape=jax.ShapeDtypeStruct((B, L, BRANCH_W), F32),
        scratch_shapes=[pltpu.VMEM((2, LANES, ATT_TQ), BF16),
                        pltpu.VMEM((2, 2, ATT_TK, ATT_TQ), F32),
                        pltpu.VMEM((2, 2, 1, ATT_TQ), F32),
                        pltpu.VMEM((2, 1, ATT_TQ), F32),
                        pltpu.VMEM((2, V_ROWS, ATT_TQ), F32)],
        compiler_params=_cparams(("parallel", "parallel", "arbitrary"), 52),
        name="diff_attention",
    )(cfar, qk, qk, vt, bias, lq1, lk1, lq2, lk2, sg)


def _fft_consts(n2):
    n1 = FFT_N1
    n = n1 * n2
    f2 = np.exp(-2j * np.pi * np.outer(np.arange(n2), np.arange(n2)) / n2)
    tw = np.exp(-2j * np.pi * np.outer(np.arange(n2), np.arange(n1)) / n)
    f1 = np.exp(-2j * np.pi * np.outer(np.arange(n1), np.arange(n1)) / n1)
    c = {}
    c["f2_full"] = np.concatenate([f2.real, f2.imag], 0)
    c["f2_half"] = c["f2_full"][:, : n2 // 2]
    c["tre"], c["tim"] = tw.real, tw.imag
    c["g_top"] = np.concatenate([f1.real, f1.imag], 1)
    c["g_bot"] = np.concatenate([-f1.imag, f1.real], 1)
    c["gi_top"] = np.concatenate([f1.real, -f1.imag], 1)
    c["gi_bot"] = np.concatenate([f1.imag, f1.real], 1)
    c["fin_re"] = f2.real.T[: n2 // 2]
    c["fin_im"] = f2.imag.T[: n2 // 2]
    out = {}
    for k, v in c.items():
        out[k] = jnp.asarray(v, F32 if k in ("tre", "tim") else BF16)
    return out


def _filter_td_kernel(emb_ref, t_ref, w1_ref, b1_ref, f0_ref, w2_ref, b2_ref, f1_ref, w3_ref, ad_ref, o_ref):
    emb = emb_ref[...].astype(BF16)
    h = jnp.dot(w1_ref[...], emb, preferred_element_type=F32) + b1_ref[...]
    h = jnp.sin(f0_ref[...] * h)
    h = jnp.dot(w2_ref[...], h.astype(BF16), preferred_element_type=F32) + b2_ref[...]
    h = jnp.sin(f1_ref[...] * h)
    h = jnp.dot(w3_ref[...], h.astype(BF16), preferred_element_type=F32)
    decay = jnp.exp(-(ad_ref[...] * t_ref[...]))
    o_ref[...] = h * decay


def _filter_td(emb_t, t_full, w1t, b1, f0, w2t, b2, f1, w3t, absdelta):
    n = emb_t.shape[1]
    pt = 2048
    nch = n // pt
    col = lambda r: pl.BlockSpec((r, 1), lambda i: (0, 0))
    return pl.pallas_call(
        _filter_td_kernel,
        grid=(nch,),
        in_specs=[pl.BlockSpec((HY_EMB_PAD, pt), lambda i: (0, i)),
                  pl.BlockSpec((1, pt), lambda i: (0, i)),
                  pl.BlockSpec((HY_ORDER, HY_EMB_PAD), lambda i: (0, 0)), col(HY_ORDER), col(HY_ORDER),
                  pl.BlockSpec((HY_ORDER, HY_ORDER), lambda i: (0, 0)), col(HY_ORDER), col(HY_ORDER),
                  pl.BlockSpec((None, BRANCH_W, HY_ORDER), lambda i: (i // (nch // 2), 0, 0)),
                  col(BRANCH_W)],
        out_specs=pl.BlockSpec((BRANCH_W, pt), lambda i: (0, i)),
        out_shape=jax.ShapeDtypeStruct((BRANCH_W, n), F32),
        compiler_params=_cparams(("parallel",), 32),
        name="hyena_filter_time",
    )(emb_t, t_full, w1t, b1, f0, w2t, b2, f1, w3t, absdelta)


def _cmul(are, aim, bre, bim):
    return are * bre - aim * bim, are * bim + aim * bre


def _filter_fft_kernel(f_ref, f2_ref, tre_ref, tim_ref, gt_ref, gb_ref, o_ref, bre_ref, bim_ref, *, cc, n2):
    n_total = FFT_N1 * n2
    row = lax.broadcasted_iota(jnp.int32, (n2, FFT_N1), 0)
    lane = lax.broadcasted_iota(jnp.int32, (n2, FFT_N1), 1)
    is_lag0_backward = (row == n2 // 2) & (lane == 0)

    def per_channel(c, carry):
        f = f_ref[c]
        s = jnp.sum(jnp.abs(f), keepdims=True)
        f = jnp.where(is_lag0_backward, 0.0, f) * (1.0 / ((s + EPS) * n_total))
        a = jnp.dot(f2_ref[...], f.astype(BF16), preferred_element_type=F32)
        bre, bim = _cmul(a[:n2], a[n2:], tre_ref[...], tim_ref[...])
        bre_ref[c] = bre.astype(BF16)
        bim_ref[c] = bim.astype(BF16)
        return carry

    lax.fori_loop(0, cc, per_channel, 0, unroll=HY_CH_UNROLL)
    bre = bre_ref[...].reshape(cc * n2, FFT_N1)
    bim = bim_ref[...].reshape(cc * n2, FFT_N1)
    o_ref[...] = (jnp.dot(bre, gt_ref[...], preferred_element_type=F32)
                  + jnp.dot(bim, gb_ref[...], preferred_element_type=F32))


def _filter_fft(filt, consts, n2, cc):
    C = filt.shape[0]
    full = lambda a: pl.BlockSpec(a.shape, lambda i: (0,) * a.ndim)
    mats = [consts["f2_full"], consts["tre"], consts["tim"], consts["g_top"], consts["g_bot"]]
    return pl.pallas_call(
        functools.partial(_filter_fft_kernel, cc=cc, n2=n2),
        grid=(C // cc,),
        in_specs=[pl.BlockSpec((cc, n2, FFT_N1), lambda i: (i, 0, 0))] + [full(a) for a in mats],
        out_specs=pl.BlockSpec((cc * n2, 2 * FFT_N1), lambda i: (i, 0)),
        out_shape=jax.ShapeDtypeStruct((C * n2, 2 * FFT_N1), F32),
        scratch_shapes=[pltpu.VMEM((cc, n2, FFT_N1), BF16), pltpu.VMEM((cc, n2, FFT_N1), BF16)],
        compiler_params=_cparams(("parallel",), 40),
        name="hyena_filter_fft",
    )(filt, *mats)


def _shift_prev(x, row, lane):
    a = pltpu.roll(x, 1, axis=1)
    b = pltpu.roll(a, 1, axis=0)
    return jnp.where(lane == 0, jnp.where(row == 0, 0.0, b), a)


def _shift_next(x, row, lane, rows):
    a = pltpu.roll(x, FFT_N1 - 1, axis=1)
    b = pltpu.roll(a, rows - 1, axis=0)
    return jnp.where(lane == FFT_N1 - 1, jnp.where(row == rows - 1, 0.0, b), a)


def _hyena_kernel(sw_ref, sb_ref, d_ref, x0_ref, x1_ref, v_ref, hf_ref, f2_ref, tre_ref, tim_ref,
                  gt_ref, gb_ref, git_ref, gib_ref, fre_ref, fim_ref, o_ref,
                  z_ref, x0c_ref, bre_ref, bim_ref, c_ref, *, cc, n2):
    rows = n2 // 2
    ch0 = pl.program_id(1) * cc
    row = lax.broadcasted_iota(jnp.int32, (rows, FFT_N1), 0)
    lane = lax.broadcasted_iota(jnp.int32, (rows, FFT_N1), 1)

    def short_conv(x, ch):
        return (sw_ref[0, ch] * _shift_prev(x, row, lane) + sw_ref[1, ch] * x
                + sw_ref[2, ch] * _shift_next(x, row, lane, rows) + sb_ref[ch])

    def forward(c, carry):
        ch = ch0 + c
        x0c_ref[c] = short_conv(x0_ref[c], ch)
        z = short_conv(x1_ref[c], BRANCH_W + ch) * short_conv(v_ref[c], 2 * BRANCH_W + ch)
        z_ref[c] = z
        a = jnp.dot(f2_ref[...], z.astype(BF16), preferred_element_type=F32)
        bre, bim = _cmul(a[:n2], a[n2:], tre_ref[...], tim_ref[...])
        bre_ref[c] = bre.astype(BF16)
        bim_ref[c] = bim.astype(BF16)
        return carry

    lax.fori_loop(0, cc, forward, 0, unroll=HY_CH_UNROLL)

    bre = bre_ref[...].reshape(cc * n2, FFT_N1)
    bim = bim_ref[...].reshape(cc * n2, FFT_N1)
    x = (jnp.dot(bre, gt_ref[...], preferred_element_type=F32)
         + jnp.dot(bim, gb_ref[...], preferred_element_type=F32))
    hf = hf_ref[...]
    yre, yim = _cmul(x[:, :FFT_N1], x[:, FFT_N1:], hf[:, :FFT_N1], hf[:, FFT_N1:])
    cfull = (jnp.dot(yre.astype(BF16), git_ref[...], preferred_element_type=F32)
             + jnp.dot(yim.astype(BF16), gib_ref[...], preferred_element_type=F32))
    c_ref[...] = cfull.reshape(cc, n2, 2 * FFT_N1)

    def backward(c, carry):
        cm = c_ref[c]
        cre, cim = cm[:, :FFT_N1], cm[:, FFT_N1:]
        tre, tim = tre_ref[...], tim_ref[...]
        dre = cre * tre + cim * tim
        dim = cim * tre - cre * tim
        y = (jnp.dot(fre_ref[...], dre.astype(BF16), preferred_element_type=F32)
             + jnp.dot(fim_ref[...], dim.astype(BF16), preferred_element_type=F32))
        o_ref[c] = x0c_ref[c] * (y + z_ref[c] * d_ref[ch0 + c])
        return carry

    lax.fori_loop(0, cc, backward, 0, unroll=HY_CH_UNROLL)


def _hyena(u, hf, consts, short_w, short_b, d_bias, n2, cc):
    B = u.shape[0]
    rows = n2 // 2
    C = BRANCH_W
    nblk = C // cc
    smem = pl.BlockSpec(memory_space=pltpu.SMEM)
    full = lambda a: pl.BlockSpec(a.shape, lambda b, i: (0,) * a.ndim)
    mats = [consts[k] for k in ("f2_half", "tre", "tim", "g_top", "g_bot", "gi_top", "gi_bot", "fin_re", "fin_im")]
    sec = lambda s: pl.BlockSpec((None, cc, rows, FFT_N1), lambda b, i: (b, s * nblk + i, 0, 0))
    return pl.pallas_call(
        functools.partial(_hyena_kernel, cc=cc, n2=n2),
        grid=(B, nblk),
        in_specs=[smem, smem, smem, sec(0), sec(1), sec(2),
                  pl.BlockSpec((cc * n2, 2 * FFT_N1), lambda b, i: (i, 0))] + [full(a) for a in mats],
        out_specs=pl.BlockSpec((None, cc, rows, FFT_N1), lambda b, i: (b, i, 0, 0)),
        out_shape=jax.ShapeDtypeStruct((B, C, rows, FFT_N1), F32),
        scratch_shapes=[pltpu.VMEM((cc, rows, FFT_N1), F32), pltpu.VMEM((cc, rows, FFT_N1), F32),
                        pltpu.VMEM((cc, n2, FFT_N1), BF16), pltpu.VMEM((cc, n2, FFT_N1), BF16),
                        pltpu.VMEM((cc, n2, 2 * FFT_N1), F32)],
        compiler_params=_cparams(("parallel", "arbitrary"), 40),
        name="hyena_mixer",
    )(short_w, short_b, d_bias, u, u, u, hf, *mats)


def _hyena_tables(L):
    t = jnp.linspace(0.0, 1.0, L, dtype=F32)[:, None]
    w = 2.0 * math.pi * jnp.arange(L, dtype=F32)[:, None] / L
    bands = jnp.linspace(1e-4, HY_BANDS - 1, HY_BANDS, dtype=F32)[None, :]
    emb = jnp.concatenate([t, jnp.cos(bands * w), -jnp.sin(bands * w)], axis=-1)
    two_sided = lambda a: jnp.concatenate([a, a[:1], a[:0:-1]], axis=0)
    emb_full = jnp.pad(two_sided(emb), ((0, 0), (0, HY_EMB_PAD - HY_EMB)))
    t_full = two_sided(t).T
    return emb_full.T, t_full


def _merge_kernel(x_ref, ya_ref, yb_ref, yct_ref, yd_ref, g_ref, ws_ref, wm_ref, wb_ref, wo_ref, fg_ref, o_ref,
                  *, final):
    x = x_ref[...]
    ms = jnp.mean(x * x, axis=-1, keepdims=True)
    h = (x * lax.rsqrt(ms + EPS) * g_ref[...]).astype(BF16)
    mixed = None
    for n in range(N_BRANCH):
        if n == 0:
            y = ya_ref[...]
        elif n == 1:
            y = yb_ref[...]
        elif n == 2:
            y = yct_ref[...].T
        else:
            y = yd_ref[...]
        sg = jnp.dot(h, ws_ref[n], preferred_element_type=F32)
        u = (y * (sg * jax.nn.sigmoid(sg))).astype(BF16)
        proj = jnp.dot(u, wb_ref[n], preferred_element_type=F32)
        merge = jax.nn.sigmoid(jnp.dot(h, wm_ref[n], preferred_element_type=F32))
        mixed = merge * proj if mixed is None else mixed + merge * proj
    out = x + jnp.dot(mixed.astype(BF16), wo_ref[...], preferred_element_type=F32)
    if final:
        ms = jnp.mean(out * out, axis=-1, keepdims=True)
        out = out * lax.rsqrt(ms + EPS) * fg_ref[...]
    o_ref[...] = out


def _merge(x2, ya, yb, yct, yd, g, ws, wm, wb, wo, fg, L, final):
    T, D = x2.shape
    tm = 256
    nl = L // tm
    W = BRANCH_W
    tok = lambda c: pl.BlockSpec((tm, c), lambda i: (i, 0))
    const = lambda a: pl.BlockSpec(a.shape, lambda i: (0,) * a.ndim, pipeline_mode=pl.Buffered(1))
    return pl.pallas_call(
        functools.partial(_merge_kernel, final=final),
        grid=(T // tm,),
        in_specs=[tok(D), tok(W), tok(W),
                  pl.BlockSpec((None, W, tm), lambda i: (i // nl, 0, i % nl)),
                  tok(W), const(g), const(ws), const(wm), const(wb), const(wo), const(fg)],
        out_specs=tok(D),
        out_shape=jax.ShapeDtypeStruct((T, D), F32),
        compiler_params=_cparams(("parallel",), 52),
        name="gated_merge",
    )(x2, ya, yb, yct, yd, g, ws, wm, wb, wo, fg)


def _layer_weights(l, norm_g, w_in, pool_w, pool_scale, subln_g, rel_bias, hy_short_w, hy_short_b, hy_w1, hy_b1,
                   hy_freq, hy_w2, hy_b2, hy_w3, hy_d, cf_dw_w, cf_dw_b, cf_ln_g, cf_ln_b, w_branch, w_out):
    w = w_in[l].astype(BF16)
    W = BRANCH_W
    p = {}
    p["g"] = norm_g[l][None, :]
    p["w_tok"] = jnp.concatenate([w[:, OFF_POOL:OFF_V], w[:, OFF_CF:OFF_SILU]], axis=1)
    p["w_chan"] = w[:, OFF_V:OFF_CF].T
    p["w_silu"] = w[:, OFF_SILU:OFF_MERGE].reshape(D_MODEL, N_BRANCH, W).transpose(1, 0, 2)
    p["w_merge"] = w[:, OFF_MERGE:].reshape(D_MODEL, N_BRANCH, D_MODEL).transpose(1, 0, 2)
    p["w_branch"] = w_branch[l].astype(BF16)
    p["w_out"] = w_out[l].astype(BF16)
    p["pool_w"] = pool_w[l].astype(BF16)
    p["pool_scale"] = pool_scale[l][None, :]
    p["subln_g"] = subln_g[l][:, None]
    buckets = jnp.asarray(_bias_buckets())[None]
    table_t = rel_bias.astype(F32).T * LOG2E
    bias = jnp.zeros((DA_HEADS,) + buckets.shape[1:], F32)
    for b in range(REL_BUCKETS):
        bias = jnp.where(buckets == b, table_t[:, b][:, None, None, None, None], bias)
    p["bias"] = bias
    p["cfar"] = jnp.stack([table_t[:, REL_BUCKETS // 2 - 1], table_t[:, REL_BUCKETS - 1]], axis=1)
    p["short_w"] = hy_short_w[l]
    p["short_b"] = hy_short_b[l]
    p["hy_d"] = hy_d[l]
    p["w1t"] = jnp.pad(hy_w1[l].T, ((0, 0), (0, HY_EMB_PAD - HY_EMB))).astype(BF16)
    p["b1"] = hy_b1[l][:, None]
    p["f0"] = hy_freq[l, 0][:, None]
    p["w2t"] = hy_w2[l].T.astype(BF16)
    p["b2"] = hy_b2[l][:, None]
    p["f1"] = hy_freq[l, 1][:, None]
    p["w3t"] = hy_w3[l].T.reshape(2, W, HY_ORDER).astype(BF16)
    max_decay = math.log(HY_DECAY_TARGET) / HY_FAST
    min_decay = math.log(HY_DECAY_TARGET) / HY_SLOW
    p["absdelta"] = jnp.abs(jnp.linspace(min_decay, max_decay, W, dtype=F32))[:, None]
    p["dw_w"] = jnp.broadcast_to(cf_dw_w[l][:, None, :], (CF_WIDTH, SUBLANES, BRANCH_W))
    p["dw_b"] = cf_dw_b[l][None, :]
    p["ln_g"] = cf_ln_g[l][None, :]
    p["ln_b"] = cf_ln_b[l][None, :]
    return p


def _hyena_chunk(n2):
    return max(1, 512 // n2)


def _filter_spectrum(p, L):
    n2 = 2 * L // FFT_N1
    consts = _fft_consts(n2)
    emb_t, t_full = _hyena_tables(L)
    filt = _filter_td(emb_t, t_full, p["w1t"], p["b1"], p["f0"], p["w2t"], p["b2"], p["f1"], p["w3t"],
                      p["absdelta"])
    hf = _filter_fft(filt.reshape(BRANCH_W, n2, FFT_N1), consts, n2, _hyena_chunk(n2))
    return hf, consts, n2


def _layer(x, p, l, lq1, lk1, lq2, lk2, final_g, final):
    B, L, D = x.shape
    T = B * L
    x2 = x.reshape(T, D)
    g = p["g"]
    lam_init = 0.8 - 0.6 * math.exp(-0.3 * l)

    pool_in, qk, cf, vt, hyt = _project(x2, g, p["w_tok"], p["w_chan"], B, L)
    qk = qk.reshape(B, L, 2 * BRANCH_W)

    ya = _pool(pool_in, p["pool_w"], p["pool_scale"], L)
    yb = _attention(qk, vt, p["bias"], p["cfar"], lq1[l][None, :], lk1[l][None, :], lq2[l][None, :],
                    lk2[l][None, :], p["subln_g"], lam_init).reshape(T, BRANCH_W)
    hf, consts, n2 = _filter_spectrum(p, L)
    yct = _hyena(hyt.reshape(B, 3 * BRANCH_W, n2 // 2, FFT_N1), hf, consts, p["short_w"], p["short_b"], p["hy_d"],
                 n2, _hyena_chunk(n2)).reshape(B, BRANCH_W, L)
    yd = _conformer(cf, p["dw_w"], p["dw_b"], p["ln_g"], p["ln_b"], L)

    out = _merge(x2, ya, yb, yct, yd, g, p["w_silu"], p["w_merge"], p["w_branch"], p["w_out"], final_g[None, :], L,
                 final)
    return out.reshape(B, L, D)


def kernel(x_prompt, x_sample, norm_g, w_in, pool_w, pool_scale, lambda_q1, lambda_k1, lambda_q2, lambda_k2, subln_g, rel_bias, hy_short_w, hy_short_b, hy_w1, hy_b1, hy_freq, hy_w2, hy_b2, hy_w3, hy_d, cf_dw_w, cf_dw_b, cf_ln_g, cf_ln_b, w_branch, w_out, final_g):
    depth = norm_g.shape[0]
    hp, hs = x_prompt, x_sample
    for l in range(depth):
        p = _layer_weights(l, norm_g, w_in, pool_w, pool_scale, subln_g, rel_bias, hy_short_w, hy_short_b, hy_w1,
                           hy_b1, hy_freq, hy_w2, hy_b2, hy_w3, hy_d, cf_dw_w, cf_dw_b, cf_ln_g, cf_ln_b, w_branch,
                           w_out)
        final = l == depth - 1
        hp = _layer(hp, p, l, lambda_q1, lambda_k1, lambda_q2, lambda_k2, final_g, final)
        hs = _layer(hs, p, l, lambda_q1, lambda_k1, lambda_q2, lambda_k2, final_g, final)
    return (hp, hs)
```

```python
import functools
import math

import numpy as np
import jax
import jax.numpy as jnp
from jax import lax
from jax.experimental import pallas as pl
from jax.experimental.pallas import tpu as pltpu

F32 = jnp.float32
BF16 = jnp.bfloat16

D_MODEL = 1024
BRANCH_W = 512
N_BRANCH = 4
POOL_WINDOWS = (2, 4, 8, 16)
POOL_GW = 128
DA_HEADS = 4
DA_HEAD_DIM = 64
DA_V_DIM = 128
REL_BUCKETS = 32
REL_MAX_DIST = 128
HY_EMB = 33
HY_EMB_PAD = 40
HY_BANDS = 16
HY_ORDER = 64
HY_DECAY_TARGET = 1e-2
HY_FAST = 0.3
HY_SLOW = 1.5
CF_WIDTH = 31
EPS = 1e-6

OFF_POOL = 0
OFF_Q = 512
OFF_K = 1024
OFF_V = 1536
OFF_HY = 2048
OFF_CF = 3584
OFF_SILU = 4608
OFF_MERGE = 6656

LANES = 128
SUBLANES = 8
CONF_ROWS = 32
FFT_N1 = 256
HALO = 16
MERGE_TILE = 256
ATT_TQ = 512
ATT_TK = 512
ATT_UNROLL = 4
ATT_NEAR = 4
HY_CH_UNROLL = 4
NEG_BIG = -1e30
LOG2E = math.log2(math.e)
V_ROWS = DA_V_DIM + 16


def _cparams(sem, vmem_mb, flags=None):
    return pltpu.CompilerParams(dimension_semantics=sem, vmem_limit_bytes=vmem_mb << 20, flags=flags)


def _proj_kernel(x_ref, g_ref, wtok_ref, wchan_ref, pool_ref, qk_ref, cf_ref, vt_ref, hyt_ref):
    x = x_ref[...]
    ms = jnp.mean(x * x, axis=-1, keepdims=True)
    h = (x * lax.rsqrt(ms + EPS) * g_ref[...]).astype(BF16)
    W = BRANCH_W

    def tok(lo, hi):
        return jnp.dot(h, wtok_ref[:, lo:hi], preferred_element_type=F32)

    def chan(lo, hi):
        return lax.dot_general(wchan_ref[lo:hi, :], h, (((1,), (1,)), ((), ())), preferred_element_type=F32)

    pool_ref[...] = tok(0, W)
    qk_ref[:, pl.ds(0, W)] = (tok(W, 2 * W) * (LOG2E * DA_HEAD_DIM ** -0.5)).astype(BF16)
    qk_ref[:, pl.ds(W, W)] = tok(2 * W, 3 * W).astype(BF16)
    cf_ref[...] = tok(3 * W, 5 * W)
    for hd in range(DA_HEADS):
        vt_ref[hd, 0, pl.ds(0, DA_V_DIM), :] = chan(hd * DA_V_DIM, (hd + 1) * DA_V_DIM).astype(BF16)
        vt_ref[hd, 0, pl.ds(DA_V_DIM, V_ROWS - DA_V_DIM), :] = jnp.ones((V_ROWS - DA_V_DIM, x.shape[0]), BF16)
    for s in range(3):
        hyt_ref[pl.ds(s * W, W), :] = chan((1 + s) * W, (2 + s) * W)


def _project(x2, g, wtok, wchan, B, L):
    T, D = x2.shape
    tm = ATT_TK
    nl = L // tm
    W = BRANCH_W
    const = lambda a: pl.BlockSpec(a.shape, lambda i: (0,) * a.ndim, pipeline_mode=pl.Buffered(1))
    tok = lambda c: pl.BlockSpec((tm, c), lambda i: (i, 0))
    return pl.pallas_call(
        _proj_kernel,
        grid=(T // tm,),
        in_specs=[tok(D), const(g), const(wtok), const(wchan)],
        out_specs=[tok(W), tok(2 * W), tok(2 * W),
                   pl.BlockSpec((None, DA_HEADS, 1, V_ROWS, tm), lambda i: (i // nl, 0, i % nl, 0, 0)),
                   pl.BlockSpec((None, 3 * W, tm), lambda i: (i // nl, 0, i % nl))],
        out_shape=[jax.ShapeDtypeStruct((T, W), F32), jax.ShapeDtypeStruct((T, 2 * W), BF16),
                   jax.ShapeDtypeStruct((T, 2 * W), F32),
                   jax.ShapeDtypeStruct((B, DA_HEADS, nl, V_ROWS, tm), BF16),
                   jax.ShapeDtypeStruct((B, 3 * W, L), F32)],
        compiler_params=_cparams(("parallel",), 48),
        name="input_projection",
    )(x2, g, wtok, wchan)


def _halo_specs(tm, C, T):
    r = tm // HALO
    last = T // HALO - 1
    return [pl.BlockSpec((tm, C), lambda i: (i, 0)),
            pl.BlockSpec((HALO, C), lambda i: (jnp.maximum(i * r - 1, 0), 0)),
            pl.BlockSpec((HALO, C), lambda i: (jnp.minimum((i + 1) * r, last), 0))]


def _fill_ext(ext_ref, cur_ref, prev_ref, next_ref, tm, L):
    i = pl.program_id(0)
    tiles_per_seq = L // tm
    first = (i % tiles_per_seq) == 0
    lastt = (i % tiles_per_seq) == tiles_per_seq - 1
    ext_ref[pl.ds(0, HALO), :] = jnp.where(first, 0.0, prev_ref[...])
    ext_ref[pl.ds(HALO, tm), :] = cur_ref[...]
    ext_ref[pl.ds(HALO + tm, HALO), :] = jnp.where(lastt, 0.0, next_ref[...])


def _pool_body(ext_ref, w_ref, sc_ref, o_ref, tm, L):
    i = pl.program_id(0)
    pos = (i % (L // tm)) * tm + lax.broadcasted_iota(jnp.int32, (tm, 1), 0)
    for g, w in enumerate(POOL_WINDOWS):
        lanes = pl.ds(g * POOL_GW, POOL_GW)
        s = None
        for d in range(-(w // 2), w - w // 2):
            t = ext_ref[pl.ds(HALO + d, tm), lanes]
            s = t if s is None else s + t
        lo = jnp.maximum(pos - w // 2, 0)
        hi = jnp.minimum(pos + (w - 1 - w // 2), L - 1)
        cnt = (hi - lo + 1).astype(F32)
        pooled = s / cnt - ext_ref[pl.ds(HALO, tm), lanes]
        mixed = jnp.dot(pooled.astype(BF16), w_ref[g], preferred_element_type=F32)
        o_ref[:, lanes] = mixed * sc_ref[:, lanes]


def _conf_prepare(ext_ref, hs_ref, tm):
    a = ext_ref[:, pl.ds(0, BRANCH_W)]
    gt = ext_ref[:, pl.ds(BRANCH_W, BRANCH_W)]
    hs_ref[0] = a * jax.nn.sigmoid(gt)
    shifted_rows = tm + 2 * HALO - SUBLANES
    for b in range(1, SUBLANES):
        hs_ref[b, pl.ds(0, shifted_rows), :] = hs_ref[0, pl.ds(b, shifted_rows), :]


def _conf_taps(hs_ref, w_ref, b_ref, o_ref, row_blocks):
    grp = (CONF_ROWS // SUBLANES, SUBLANES, BRANCH_W)
    for rb in row_blocks:
        r0 = rb * CONF_ROWS
        acc = jnp.zeros(grp, F32)
        for j in range(CF_WIDTH):
            a8, b = divmod(HALO - CF_WIDTH // 2 + j, SUBLANES)
            acc = acc + w_ref[j] * hs_ref[b, pl.ds(r0 + SUBLANES * a8, CONF_ROWS), :].reshape(grp)
        o_ref[pl.ds(r0, CONF_ROWS), :] = acc.reshape(CONF_ROWS, BRANCH_W) + b_ref[...]


def _conf_finish(o_ref, g_ref, beta_ref):
    acc = o_ref[...]
    mu = jnp.mean(acc, axis=-1, keepdims=True)
    cen = acc - mu
    var = jnp.mean(cen * cen, axis=-1, keepdims=True)
    y = cen * lax.rsqrt(var + EPS) * g_ref[...] + beta_ref[...]
    o_ref[...] = y * jax.nn.sigmoid(y)


def _rel_bucket_np(rel):
    nb = REL_BUCKETS // 2
    max_exact = nb // 2
    ret = np.where(rel > 0, nb, 0)
    n = np.abs(rel)
    ratio = np.log(np.maximum(n, 1).astype(np.float32) / np.float32(max_exact)) / np.float32(
        math.log(REL_MAX_DIST / max_exact))
    large = max_exact + (ratio * np.float32(nb - max_exact)).astype(np.int32)
    large = np.minimum(large, nb - 1)
    return ret + np.where(n < max_exact, n, large)


def _bias_buckets():
    r = np.arange(ATT_TK)[:, None]
    c = np.arange(ATT_TQ)[None, :]
    span = ATT_NEAR - 1
    out = np.zeros((ATT_TK // ATT_TQ, 2 * span + 1, ATT_TK, ATT_TQ), np.int32)
    for e in range(ATT_TK // ATT_TQ):
        for dj in range(-span, span + 1):
            out[e, dj + span] = _rel_bucket_np(ATT_TK * dj + r - ATT_TQ * e - c)
        assert (out[e, :span - 1] == REL_BUCKETS // 2 - 1).all() and (out[e, span + 2:] == REL_BUCKETS - 1).all()
    return out


def _attn_kernel(cfar_ref, q_ref, k_ref, vt_ref, bias_ref, lq1_ref, lk1_ref, lq2_ref, lk2_ref, sg_ref,
                 o_ref, qt_ref, s_ref, smax_ref, m_ref, acc_ref, *, nk, lam_init):
    h = pl.program_id(1)
    qi = pl.program_id(2)
    qt = q_ref[...].astype(F32).T
    feat = lax.broadcasted_iota(jnp.int32, qt.shape, 0)
    qt_ref[0] = jnp.where(feat < DA_HEAD_DIM, qt, 0.0).astype(BF16)
    qt_ref[1] = jnp.where(feat >= DA_HEAD_DIM, qt, 0.0).astype(BF16)
    m_ref[...] = jnp.full(m_ref.shape, NEG_BIG, F32)
    acc_ref[...] = jnp.zeros(acc_ref.shape, F32)
    j0 = qi // (ATT_TK // ATT_TQ)
    e = qi % (ATT_TK // ATT_TQ)
    c_left = cfar_ref[h, 0]
    c_right = cfar_ref[h, 1]

    def scores(j, slot, near_kind=None):
        kblk = k_ref[pl.ds(pl.multiple_of(j * ATT_TK, ATT_TK), ATT_TK), :]
        for mm in range(2):
            s = jnp.dot(kblk, qt_ref[mm], preferred_element_type=F32)
            if near_kind is not None:
                s = s + bias_ref[e, near_kind]
            s_ref[slot, mm] = s
            smax_ref[slot, mm] = jnp.max(s, axis=0, keepdims=True)

    def softmax_values(j, slot, off):
        vblk = vt_ref[j]
        for mm in range(2):
            m_prev = m_ref[mm]
            m_new = jnp.maximum(m_prev, smax_ref[slot, mm] + off)
            p = jnp.exp2(s_ref[slot, mm] - (m_new - off)).astype(BF16)
            acc_ref[mm] = (jnp.exp2(m_prev - m_new) * acc_ref[mm]
                           + jnp.dot(vblk, p, preferred_element_type=F32))
            m_ref[mm] = m_new

    near_lo = jnp.clip(j0 - 1, 0, nk - ATT_NEAR)
    n_far = nk - ATT_NEAR

    def far_chunk(t):
        t = jnp.minimum(t, n_far - 1)
        return jnp.where(t < near_lo, t, t + ATT_NEAR)

    def far_offset(j):
        return jnp.where(j < j0, c_left, c_right)

    scores(near_lo, 0, near_lo - j0 + ATT_NEAR - 1)
    for i in range(ATT_NEAR):
        j = near_lo + i
        if i + 1 < ATT_NEAR:
            scores(j + 1, (i + 1) % 2, j + 1 - j0 + ATT_NEAR - 1)
        else:
            scores(far_chunk(0), (i + 1) % 2)
        softmax_values(j, i % 2, 0.0)

    def body(i, carry):
        for u in range(ATT_UNROLL):
            t = ATT_UNROLL * i + u
            cur = u % 2
            scores(far_chunk(t + 1), 1 - cur)
            j = far_chunk(t)
            softmax_values(j, cur, far_offset(j))
        return carry

    lax.fori_loop(0, n_far // ATT_UNROLL, body, 0)

    lam = (jnp.exp(jnp.sum(lq1_ref[...] * lk1_ref[...], keepdims=True))
           - jnp.exp(jnp.sum(lq2_ref[...] * lk2_ref[...], keepdims=True)) + lam_init)
    a0 = acc_ref[0]
    a1 = acc_ref[1]
    o = (a0[:DA_V_DIM] / a0[DA_V_DIM:DA_V_DIM + 1]
         - lam * (a1[:DA_V_DIM] / a1[DA_V_DIM:DA_V_DIM + 1]))
    ms = jnp.mean(o * o, axis=0, keepdims=True)
    y = o * lax.rsqrt(ms + EPS) * sg_ref[...] * (1.0 - lam_init)
    o_ref[...] = y.T


def _attention(qk, vt, bias, cfar, lq1, lk1, lq2, lk2, sg, lam_init):
    B, L, _ = qk.shape
    nk = L // ATT_TK
    assert (nk - ATT_NEAR) % ATT_UNROLL == 0 and ATT_UNROLL % 2 == 0 and ATT_NEAR % 2 == 0 and nk > ATT_NEAR
    vec = pl.BlockSpec((1, DA_HEAD_DIM), lambda b, h, i: (0, 0))
    return pl.pallas_call(
        functools.partial(_attn_kernel, nk=nk, lam_init=lam_init),
        grid=(B, DA_HEADS, L // ATT_TQ),
        in_specs=[pl.BlockSpec(memory_space=pltpu.SMEM),
                  pl.BlockSpec((None, ATT_TQ, LANES), lambda b, h, i: (b, i, h)),
                  pl.BlockSpec((None, L, LANES), lambda b, h, i: (b, 0, DA_HEADS + h)),
                  pl.BlockSpec((None, None, nk, V_ROWS, ATT_TK), lambda b, h, i: (b, h, 0, 0, 0)),
                  pl.BlockSpec((None, ATT_TK // ATT_TQ, 2 * ATT_NEAR - 1, ATT_TK, ATT_TQ), lambda b, h, i: (h, 0, 0, 0, 0)),
                  vec, vec, vec, vec,
                  pl.BlockSpec((DA_V_DIM, 1), lambda b, h, i: (0, 0))],
        out_specs=pl.BlockSpec((None, ATT_TQ, LANES), lambda b, h, i: (b, i, h)),
        out_shape=jax.ShapeDtypeStruct((B, L, BRANCH_W), F32),
        scratch_shapes=[pltpu.VMEM((2, LANES, ATT_TQ), BF16),
                        pltpu.VMEM((2, 2, ATT_TK, ATT_TQ), F32),
                        pltpu.VMEM((2, 2, 1, ATT_TQ), F32),
                        pltpu.VMEM((2, 1, ATT_TQ), F32),
                        pltpu.VMEM((2, V_ROWS, ATT_TQ), F32)],
        compiler_params=_cparams(("parallel", "parallel", "arbitrary"), 52),
        name="diff_attention",
    )(cfar, qk, qk, vt, bias, lq1, lk1, lq2, lk2, sg)


def _fft_consts(n2):
    n1 = FFT_N1
    n = n1 * n2
    f2 = np.exp(-2j * np.pi * np.outer(np.arange(n2), np.arange(n2)) / n2)
    tw = np.exp(-2j * np.pi * np.outer(np.arange(n2), np.arange(n1)) / n)
    f1 = np.exp(-2j * np.pi * np.outer(np.arange(n1), np.arange(n1)) / n1)
    c = {}
    c["f2_full"] = np.concatenate([f2.real, f2.imag], 0)
    c["f2_half"] = c["f2_full"][:, : n2 // 2]
    c["tre"], c["tim"] = tw.real, tw.imag
    c["g_top"] = np.concatenate([f1.real, f1.imag], 1)
    c["g_bot"] = np.concatenate([-f1.imag, f1.real], 1)
    c["gi_top"] = np.concatenate([f1.real, -f1.imag], 1)
    c["gi_bot"] = np.concatenate([f1.imag, f1.real], 1)
    c["fin_re"] = f2.real.T[: n2 // 2]
    c["fin_im"] = f2.imag.T[: n2 // 2]
    out = {}
    for k, v in c.items():
        out[k] = jnp.asarray(v, F32 if k in ("tre", "tim") else BF16)
    return out


def _filter_td_kernel(emb_ref, t_ref, w1_ref, b1_ref, f0_ref, w2_ref, b2_ref, f1_ref, w3_ref, ad_ref, o_ref):
    emb = emb_ref[...].astype(BF16)
    h = jnp.dot(w1_ref[...], emb, preferred_element_type=F32) + b1_ref[...]
    h = jnp.sin(f0_ref[...] * h)
    h = jnp.dot(w2_ref[...], h.astype(BF16), preferred_element_type=F32) + b2_ref[...]
    h = jnp.sin(f1_ref[...] * h)
    h = jnp.dot(w3_ref[...], h.astype(BF16), preferred_element_type=F32)
    decay = jnp.exp(-(ad_ref[...] * t_ref[...]))
    o_ref[...] = h * decay


def _filter_td(emb_t, t_full, w1t, b1, f0, w2t, b2, f1, w3t, absdelta):
    n = emb_t.shape[1]
    pt = 2048
    nch = n // pt
    col = lambda r: pl.BlockSpec((r, 1), lambda i: (0, 0))
    return pl.pallas_call(
        _filter_td_kernel,
        grid=(nch,),
        in_specs=[pl.BlockSpec((HY_EMB_PAD, pt), lambda i: (0, i)),
                  pl.BlockSpec((1, pt), lambda i: (0, i)),
                  pl.BlockSpec((HY_ORDER, HY_EMB_PAD), lambda i: (0, 0)), col(HY_ORDER), col(HY_ORDER),
                  pl.BlockSpec((HY_ORDER, HY_ORDER), lambda i: (0, 0)), col(HY_ORDER), col(HY_ORDER),
                  pl.BlockSpec((None, BRANCH_W, HY_ORDER), lambda i: (i // (nch // 2), 0, 0)),
                  col(BRANCH_W)],
        out_specs=pl.BlockSpec((BRANCH_W, pt), lambda i: (0, i)),
        out_shape=jax.ShapeDtypeStruct((BRANCH_W, n), F32),
        compiler_params=_cparams(("parallel",), 32),
        name="hyena_filter_time",
    )(emb_t, t_full, w1t, b1, f0, w2t, b2, f1, w3t, absdelta)


def _cmul(are, aim, bre, bim):
    return are * bre - aim * bim, are * bim + aim * bre


def _filter_fft_kernel(f_ref, f2_ref, tre_ref, tim_ref, gt_ref, gb_ref, o_ref, bre_ref, bim_ref, *, cc, n2):
    n_total = FFT_N1 * n2
    row = lax.broadcasted_iota(jnp.int32, (n2, FFT_N1), 0)
    lane = lax.broadcasted_iota(jnp.int32, (n2, FFT_N1), 1)
    is_lag0_backward = (row == n2 // 2) & (lane == 0)

    def per_channel(c, carry):
        f = f_ref[c]
        s = jnp.sum(jnp.abs(f), keepdims=True)
        f = jnp.where(is_lag0_backward, 0.0, f) * (1.0 / ((s + EPS) * n_total))
        a = jnp.dot(f2_ref[...], f.astype(BF16), preferred_element_type=F32)
        bre, bim = _cmul(a[:n2], a[n2:], tre_ref[...], tim_ref[...])
        bre_ref[c] = bre.astype(BF16)
        bim_ref[c] = bim.astype(BF16)
        return carry

    lax.fori_loop(0, cc, per_channel, 0, unroll=HY_CH_UNROLL)
    bre = bre_ref[...].reshape(cc * n2, FFT_N1)
    bim = bim_ref[...].reshape(cc * n2, FFT_N1)
    o_ref[...] = (jnp.dot(bre, gt_ref[...], preferred_element_type=F32)
                  + jnp.dot(bim, gb_ref[...], preferred_element_type=F32))


def _filter_fft(filt, consts, n2, cc):
    C = filt.shape[0]
    full = lambda a: pl.BlockSpec(a.shape, lambda i: (0,) * a.ndim)
    mats = [consts["f2_full"], consts["tre"], consts["tim"], consts["g_top"], consts["g_bot"]]
    return pl.pallas_call(
        functools.partial(_filter_fft_kernel, cc=cc, n2=n2),
        grid=(C // cc,),
        in_specs=[pl.BlockSpec((cc, n2, FFT_N1), lambda i: (i, 0, 0))] + [full(a) for a in mats],
        out_specs=pl.BlockSpec((cc * n2, 2 * FFT_N1), lambda i: (i, 0)),
        out_shape=jax.ShapeDtypeStruct((C * n2, 2 * FFT_N1), F32),
        scratch_shapes=[pltpu.VMEM((cc, n2, FFT_N1), BF16), pltpu.VMEM((cc, n2, FFT_N1), BF16)],
        compiler_params=_cparams(("parallel",), 40),
        name="hyena_filter_fft",
    )(filt, *mats)


def _shift_prev(x, row, lane):
    a = pltpu.roll(x, 1, axis=1)
    b = pltpu.roll(a, 1, axis=0)
    return jnp.where(lane == 0, jnp.where(row == 0, 0.0, b), a)


def _shift_next(x, row, lane, rows):
    a = pltpu.roll(x, FFT_N1 - 1, axis=1)
    b = pltpu.roll(a, rows - 1, axis=0)
    return jnp.where(lane == FFT_N1 - 1, jnp.where(row == rows - 1, 0.0, b), a)


def _hyena_kernel(sw_ref, sb_ref, d_ref, x0_ref, x1_ref, v_ref, hf_ref, f2_ref, tre_ref, tim_ref,
                  gt_ref, gb_ref, git_ref, gib_ref, fre_ref, fim_ref, o_ref,
                  z_ref, x0c_ref, bre_ref, bim_ref, c_ref, *, cc, n2):
    rows = n2 // 2
    ch0 = pl.program_id(1) * cc
    row = lax.broadcasted_iota(jnp.int32, (rows, FFT_N1), 0)
    lane = lax.broadcasted_iota(jnp.int32, (rows, FFT_N1), 1)

    def short_conv(x, ch):
        return (sw_ref[0, ch] * _shift_prev(x, row, lane) + sw_ref[1, ch] * x
                + sw_ref[2, ch] * _shift_next(x, row, lane, rows) + sb_ref[ch])

    def forward(c, carry):
        ch = ch0 + c
        x0c_ref[c] = short_conv(x0_ref[c], ch)
        z = short_conv(x1_ref[c], BRANCH_W + ch) * short_conv(v_ref[c], 2 * BRANCH_W + ch)
        z_ref[c] = z
        a = jnp.dot(f2_ref[...], z.astype(BF16), preferred_element_type=F32)
        bre, bim = _cmul(a[:n2], a[n2:], tre_ref[...], tim_ref[...])
        bre_ref[c] = bre.astype(BF16)
        bim_ref[c] = bim.astype(BF16)
        return carry

    lax.fori_loop(0, cc, forward, 0, unroll=HY_CH_UNROLL)

    bre = bre_ref[...].reshape(cc * n2, FFT_N1)
    bim = bim_ref[...].reshape(cc * n2, FFT_N1)
    x = (jnp.dot(bre, gt_ref[...], preferred_element_type=F32)
         + jnp.dot(bim, gb_ref[...], preferred_element_type=F32))
    hf = hf_ref[...]
    yre, yim = _cmul(x[:, :FFT_N1], x[:, FFT_N1:], hf[:, :FFT_N1], hf[:, FFT_N1:])
    cfull = (jnp.dot(yre.astype(BF16), git_ref[...], preferred_element_type=F32)
             + jnp.dot(yim.astype(BF16), gib_ref[...], preferred_element_type=F32))
    c_ref[...] = cfull.reshape(cc, n2, 2 * FFT_N1)

    def backward(c, carry):
        cm = c_ref[c]
        cre, cim = cm[:, :FFT_N1], cm[:, FFT_N1:]
        tre, tim = tre_ref[...], tim_ref[...]
        dre = cre * tre + cim * tim
        dim = cim * tre - cre * tim
        y = (jnp.dot(fre_ref[...], dre.astype(BF16), preferred_element_type=F32)
             + jnp.dot(fim_ref[...], dim.astype(BF16), preferred_element_type=F32))
        o_ref[c] = x0c_ref[c] * (y + z_ref[c] * d_ref[ch0 + c])
        return carry

    lax.fori_loop(0, cc, backward, 0, unroll=HY_CH_UNROLL)


def _hyena(u, hf, consts, short_w, short_b, d_bias, n2, cc):
    B = u.shape[0]
    rows = n2 // 2
    C = BRANCH_W
    nblk = C // cc
    smem = pl.BlockSpec(memory_space=pltpu.SMEM)
    full = lambda a: pl.BlockSpec(a.shape, lambda b, i: (0,) * a.ndim)
    mats = [consts[k] for k in ("f2_half", "tre", "tim", "g_top", "g_bot", "gi_top", "gi_bot", "fin_re", "fin_im")]
    sec = lambda s: pl.BlockSpec((None, cc, rows, FFT_N1), lambda b, i: (b, s * nblk + i, 0, 0))
    return pl.pallas_call(
        functools.partial(_hyena_kernel, cc=cc, n2=n2),
        grid=(B, nblk),
        in_specs=[smem, smem, smem, sec(0), sec(1), sec(2),
                  pl.BlockSpec((cc * n2, 2 * FFT_N1), lambda b, i: (i, 0))] + [full(a) for a in mats],
        out_specs=pl.BlockSpec((None, cc, rows, FFT_N1), lambda b, i: (b, i, 0, 0)),
        out_shape=jax.ShapeDtypeStruct((B, C, rows, FFT_N1), F32),
        scratch_shapes=[pltpu.VMEM((cc, rows, FFT_N1), F32), pltpu.VMEM((cc, rows, FFT_N1), F32),
                        pltpu.VMEM((cc, n2, FFT_N1), BF16), pltpu.VMEM((cc, n2, FFT_N1), BF16),
                        pltpu.VMEM((cc, n2, 2 * FFT_N1), F32)],
        compiler_params=_cparams(("parallel", "arbitrary"), 40),
        name="hyena_mixer",
    )(short_w, short_b, d_bias, u, u, u, hf, *mats)


def _hyena_tables(L):
    t = jnp.linspace(0.0, 1.0, L, dtype=F32)[:, None]
    w = 2.0 * math.pi * jnp.arange(L, dtype=F32)[:, None] / L
    bands = jnp.linspace(1e-4, HY_BANDS - 1, HY_BANDS, dtype=F32)[None, :]
    emb = jnp.concatenate([t, jnp.cos(bands * w), -jnp.sin(bands * w)], axis=-1)
    two_sided = lambda a: jnp.concatenate([a, a[:1], a[:0:-1]], axis=0)
    emb_full = jnp.pad(two_sided(emb), ((0, 0), (0, HY_EMB_PAD - HY_EMB)))
    t_full = two_sided(t).T
    return emb_full.T, t_full


def _merge_kernel(x_ref, pa_ref, pa_prev_ref, pa_next_ref, yb_ref, yct_ref, cf_ref, cf_prev_ref, cf_next_ref,
                  g_ref, ws_ref, wm_ref, wb_ref, wo_ref, fg_ref, pw_ref, psc_ref, dw_ref, db_ref, lg_ref, lb_ref,
                  o_ref, pext_ref, cext_ref, hs_ref, ya_ref, yd_ref, *, final, tm, L):
    _fill_ext(pext_ref, pa_ref, pa_prev_ref, pa_next_ref, tm, L)
    _fill_ext(cext_ref, cf_ref, cf_prev_ref, cf_next_ref, tm, L)
    _pool_body(pext_ref, pw_ref, psc_ref, ya_ref, tm, L)
    _conf_prepare(cext_ref, hs_ref, tm)
    x = x_ref[...]
    ms = jnp.mean(x * x, axis=-1, keepdims=True)
    h = (x * lax.rsqrt(ms + EPS) * g_ref[...]).astype(BF16)
    mixed = None
    n_blocks = tm // CONF_ROWS
    for step, n in enumerate((1, 2, 0, 3)):
        if step < 3:
            _conf_taps(hs_ref, dw_ref, db_ref, yd_ref, range(step * n_blocks // 3, (step + 1) * n_blocks // 3))
        else:
            _conf_finish(yd_ref, lg_ref, lb_ref)
        if n == 0:
            y = ya_ref[...]
        elif n == 1:
            y = yb_ref[...]
        elif n == 2:
            y = yct_ref[...].T
        else:
            y = yd_ref[...]
        sg = jnp.dot(h, ws_ref[n], preferred_element_type=F32)
        u = (y * (sg * jax.nn.sigmoid(sg))).astype(BF16)
        proj = jnp.dot(u, wb_ref[n], preferred_element_type=F32)
        merge = jax.nn.sigmoid(jnp.dot(h, wm_ref[n], preferred_element_type=F32))
        mixed = merge * proj if mixed is None else mixed + merge * proj
    out = x + jnp.dot(mixed.astype(BF16), wo_ref[...], preferred_element_type=F32)
    if final:
        ms = jnp.mean(out * out, axis=-1, keepdims=True)
        out = out * lax.rsqrt(ms + EPS) * fg_ref[...]
    o_ref[...] = out


def _merge(x2, pool_in, yb, yct, cf, p, fg, L, final):
    T, D = x2.shape
    tm = MERGE_TILE
    nl = L // tm
    W = BRANCH_W
    tok = lambda c: pl.BlockSpec((tm, c), lambda i: (i, 0))
    const = lambda a: pl.BlockSpec(a.shape, lambda i: (0,) * a.ndim, pipeline_mode=pl.Buffered(1))
    consts = [p["g"], p["w_silu"], p["w_merge"], p["w_branch"], p["w_out"], fg, p["pool_w"], p["pool_scale"],
              p["dw_w"], p["dw_b"], p["ln_g"], p["ln_b"]]
    return pl.pallas_call(
        functools.partial(_merge_kernel, final=final, tm=tm, L=L),
        grid=(T // tm,),
        in_specs=[tok(D)] + _halo_specs(tm, W, T) + [tok(W), pl.BlockSpec((None, W, tm), lambda i: (i // nl, 0, i % nl))]
        + _halo_specs(tm, 2 * W, T) + [const(a) for a in consts],
        out_specs=tok(D),
        out_shape=jax.ShapeDtypeStruct((T, D), F32),
        scratch_shapes=[pltpu.VMEM((tm + 2 * HALO, W), F32), pltpu.VMEM((tm + 2 * HALO, 2 * W), F32),
                        pltpu.VMEM((SUBLANES, tm + 2 * HALO, W), F32), pltpu.VMEM((tm, W), F32),
                        pltpu.VMEM((tm, W), F32)],
        compiler_params=_cparams(("parallel",), 56),
        name="gated_merge",
    )(x2, pool_in, pool_in, pool_in, yb, yct, cf, cf, cf, *consts)


def _layer_weights(l, norm_g, w_in, pool_w, pool_scale, subln_g, rel_bias, hy_short_w, hy_short_b, hy_w1, hy_b1,
                   hy_freq, hy_w2, hy_b2, hy_w3, hy_d, cf_dw_w, cf_dw_b, cf_ln_g, cf_ln_b, w_branch, w_out):
    w = w_in[l].astype(BF16)
    W = BRANCH_W
    p = {}
    p["g"] = norm_g[l][None, :]
    p["w_tok"] = jnp.concatenate([w[:, OFF_POOL:OFF_V], w[:, OFF_CF:OFF_SILU]], axis=1)
    p["w_chan"] = w[:, OFF_V:OFF_CF].T
    p["w_silu"] = w[:, OFF_SILU:OFF_MERGE].reshape(D_MODEL, N_BRANCH, W).transpose(1, 0, 2)
    p["w_merge"] = w[:, OFF_MERGE:].reshape(D_MODEL, N_BRANCH, D_MODEL).transpose(1, 0, 2)
    p["w_branch"] = w_branch[l].astype(BF16)
    p["w_out"] = w_out[l].astype(BF16)
    p["pool_w"] = pool_w[l].astype(BF16)
    p["pool_scale"] = pool_scale[l][None, :]
    p["subln_g"] = subln_g[l][:, None]
    buckets = jnp.asarray(_bias_buckets())[None]
    table_t = rel_bias.astype(F32).T * LOG2E
    bias = jnp.zeros((DA_HEADS,) + buckets.shape[1:], F32)
    for b in range(REL_BUCKETS):
        bias = jnp.where(buckets == b, table_t[:, b][:, None, None, None, None], bias)
    p["bias"] = bias
    p["cfar"] = jnp.stack([table_t[:, REL_BUCKETS // 2 - 1], table_t[:, REL_BUCKETS - 1]], axis=1)
    p["short_w"] = hy_short_w[l]
    p["short_b"] = hy_short_b[l]
    p["hy_d"] = hy_d[l]
    p["w1t"] = jnp.pad(hy_w1[l].T, ((0, 0), (0, HY_EMB_PAD - HY_EMB))).astype(BF16)
    p["b1"] = hy_b1[l][:, None]
    p["f0"] = hy_freq[l, 0][:, None]
    p["w2t"] = hy_w2[l].T.astype(BF16)
    p["b2"] = hy_b2[l][:, None]
    p["f1"] = hy_freq[l, 1][:, None]
    p["w3t"] = hy_w3[l].T.reshape(2, W, HY_ORDER).astype(BF16)
    max_decay = math.log(HY_DECAY_TARGET) / HY_FAST
    min_decay = math.log(HY_DECAY_TARGET) / HY_SLOW
    p["absdelta"] = jnp.abs(jnp.linspace(min_decay, max_decay, W, dtype=F32))[:, None]
    p["dw_w"] = jnp.broadcast_to(cf_dw_w[l][:, None, :], (CF_WIDTH, SUBLANES, BRANCH_W))
    p["dw_b"] = cf_dw_b[l][None, :]
    p["ln_g"] = cf_ln_g[l][None, :]
    p["ln_b"] = cf_ln_b[l][None, :]
    return p


def _hyena_chunk(n2):
    return max(1, 512 // n2)


def _filter_spectrum(p, L):
    n2 = 2 * L // FFT_N1
    consts = _fft_consts(n2)
    emb_t, t_full = _hyena_tables(L)
    filt = _filter_td(emb_t, t_full, p["w1t"], p["b1"], p["f0"], p["w2t"], p["b2"], p["f1"], p["w3t"],
                      p["absdelta"])
    hf = _filter_fft(filt.reshape(BRANCH_W, n2, FFT_N1), consts, n2, _hyena_chunk(n2))
    return hf, consts, n2


def _layer(x, p, l, lq1, lk1, lq2, lk2, final_g, final):
    B, L, D = x.shape
    T = B * L
    x2 = x.reshape(T, D)
    g = p["g"]
    lam_init = 0.8 - 0.6 * math.exp(-0.3 * l)

    pool_in, qk, cf, vt, hyt = _project(x2, g, p["w_tok"], p["w_chan"], B, L)
    qk = qk.reshape(B, L, 2 * BRANCH_W)

    yb = _attention(qk, vt, p["bias"], p["cfar"], lq1[l][None, :], lk1[l][None, :], lq2[l][None, :],
                    lk2[l][None, :], p["subln_g"], lam_init).reshape(T, BRANCH_W)
    hf, consts, n2 = _filter_spectrum(p, L)
    yct = _hyena(hyt.reshape(B, 3 * BRANCH_W, n2 // 2, FFT_N1), hf, consts, p["short_w"], p["short_b"], p["hy_d"],
                 n2, _hyena_chunk(n2)).reshape(B, BRANCH_W, L)
    out = _merge(x2, pool_in, yb, yct, cf, p, final_g[None, :], L, final)
    return out.reshape(B, L, D)


def kernel(x_prompt, x_sample, norm_g, w_in, pool_w, pool_scale, lambda_q1, lambda_k1, lambda_q2, lambda_k2, subln_g, rel_bias, hy_short_w, hy_short_b, hy_w1, hy_b1, hy_freq, hy_w2, hy_b2, hy_w3, hy_d, cf_dw_w, cf_dw_b, cf_ln_g, cf_ln_b, w_branch, w_out, final_g):
    depth = norm_g.shape[0]
    hp, hs = x_prompt, x_sample
    for l in range(depth):
        p = _layer_weights(l, norm_g, w_in, pool_w, pool_scale, subln_g, rel_bias, hy_short_w, hy_short_b, hy_w1,
                           hy_b1, hy_freq, hy_w2, hy_b2, hy_w3, hy_d, cf_dw_w, cf_dw_b, cf_ln_g, cf_ln_b, w_branch,
                           w_out)
        final = l == depth - 1
        hp = _layer(hp, p, l, lambda_q1, lambda_k1, lambda_q2, lambda_k2, final_g, final)
        hs = _layer(hs, p, l, lambda_q1, lambda_k1, lambda_q2, lambda_k2, final_g, final)
    return (hp, hs)
```

```python
import functools
import math

import numpy as np
import jax
import jax.numpy as jnp
from jax import lax
from jax.experimental import pallas as pl
from jax.experimental.pallas import tpu as pltpu

F32 = jnp.float32
BF16 = jnp.bfloat16

D_MODEL = 1024
BRANCH_W = 512
N_BRANCH = 4
POOL_WINDOWS = (2, 4, 8, 16)
POOL_GW = 128
DA_HEADS = 4
DA_HEAD_DIM = 64
DA_V_DIM = 128
REL_BUCKETS = 32
REL_MAX_DIST = 128
HY_EMB = 33
HY_EMB_PAD = 40
HY_BANDS = 16
HY_ORDER = 64
HY_DECAY_TARGET = 1e-2
HY_FAST = 0.3
HY_SLOW = 1.5
CF_WIDTH = 31
EPS = 1e-6

OFF_POOL = 0
OFF_Q = 512
OFF_K = 1024
OFF_V = 1536
OFF_HY = 2048
OFF_CF = 3584
OFF_SILU = 4608
OFF_MERGE = 6656

LANES = 128
SUBLANES = 8
CONF_ROWS = 32
FFT_N1 = 256
HALO = 16
MERGE_TILE = 256
ATT_TQ = 512
ATT_TK = 512
ATT_UNROLL = 4
ATT_NEAR = 4
HY_CH_UNROLL = 4
NEG_BIG = -1e30
LOG2E = math.log2(math.e)
V_ROWS = DA_V_DIM + 16


def _cparams(sem, vmem_mb, flags=None):
    return pltpu.CompilerParams(dimension_semantics=sem, vmem_limit_bytes=vmem_mb << 20, flags=flags)


def _proj_kernel(x_ref, g_ref, wtok_ref, wchan_ref, pool_ref, qk_ref, cf_ref, vt_ref, hyt_ref):
    x = x_ref[...]
    ms = jnp.mean(x * x, axis=-1, keepdims=True)
    h = (x * lax.rsqrt(ms + EPS) * g_ref[...]).astype(BF16)
    W = BRANCH_W

    def tok(lo, hi):
        return jnp.dot(h, wtok_ref[:, lo:hi], preferred_element_type=F32)

    def chan(lo, hi):
        return lax.dot_general(wchan_ref[lo:hi, :], h, (((1,), (1,)), ((), ())), preferred_element_type=F32)

    pool_ref[...] = tok(0, W)
    qk_ref[:, pl.ds(0, W)] = (tok(W, 2 * W) * (LOG2E * DA_HEAD_DIM ** -0.5)).astype(BF16)
    qk_ref[:, pl.ds(W, W)] = tok(2 * W, 3 * W).astype(BF16)
    cf_ref[...] = tok(3 * W, 5 * W)
    for hd in range(DA_HEADS):
        vt_ref[hd, 0, pl.ds(0, DA_V_DIM), :] = chan(hd * DA_V_DIM, (hd + 1) * DA_V_DIM).astype(BF16)
        vt_ref[hd, 0, pl.ds(DA_V_DIM, V_ROWS - DA_V_DIM), :] = jnp.ones((V_ROWS - DA_V_DIM, x.shape[0]), BF16)
    for s in range(3):
        hyt_ref[pl.ds(s * W, W), :] = chan((1 + s) * W, (2 + s) * W)


def _project(x2, g, wtok, wchan, B, L):
    T, D = x2.shape
    tm = ATT_TK
    nl = L // tm
    W = BRANCH_W
    const = lambda a: pl.BlockSpec(a.shape, lambda i: (0,) * a.ndim, pipeline_mode=pl.Buffered(1))
    tok = lambda c: pl.BlockSpec((tm, c), lambda i: (i, 0))
    return pl.pallas_call(
        _proj_kernel,
        grid=(T // tm,),
        in_specs=[tok(D), const(g), const(wtok), const(wchan)],
        out_specs=[tok(W), tok(2 * W), tok(2 * W),
                   pl.BlockSpec((None, DA_HEADS, 1, V_ROWS, tm), lambda i: (i // nl, 0, i % nl, 0, 0)),
                   pl.BlockSpec((None, 3 * W, tm), lambda i: (i // nl, 0, i % nl))],
        out_shape=[jax.ShapeDtypeStruct((T, W), F32), jax.ShapeDtypeStruct((T, 2 * W), BF16),
                   jax.ShapeDtypeStruct((T, 2 * W), F32),
                   jax.ShapeDtypeStruct((B, DA_HEADS, nl, V_ROWS, tm), BF16),
                   jax.ShapeDtypeStruct((B, 3 * W, L), F32)],
        compiler_params=_cparams(("parallel",), 48),
        name="input_projection",
    )(x2, g, wtok, wchan)


def _halo_specs(tm, C, T):
    r = tm // HALO
    last = T // HALO - 1
    return [pl.BlockSpec((tm, C), lambda i: (i, 0)),
            pl.BlockSpec((HALO, C), lambda i: (jnp.maximum(i * r - 1, 0), 0)),
            pl.BlockSpec((HALO, C), lambda i: (jnp.minimum((i + 1) * r, last), 0))]


def _fill_ext(ext_ref, cur_ref, prev_ref, next_ref, tm, L):
    i = pl.program_id(0)
    tiles_per_seq = L // tm
    first = (i % tiles_per_seq) == 0
    lastt = (i % tiles_per_seq) == tiles_per_seq - 1
    ext_ref[pl.ds(0, HALO), :] = jnp.where(first, 0.0, prev_ref[...])
    ext_ref[pl.ds(HALO, tm), :] = cur_ref[...]
    ext_ref[pl.ds(HALO + tm, HALO), :] = jnp.where(lastt, 0.0, next_ref[...])


def _pool_body(ext_ref, w_ref, sc_ref, o_ref, tm, L):
    i = pl.program_id(0)
    pos = (i % (L // tm)) * tm + lax.broadcasted_iota(jnp.int32, (tm, 1), 0)
    for g, w in enumerate(POOL_WINDOWS):
        lanes = pl.ds(g * POOL_GW, POOL_GW)
        s = None
        for d in range(-(w // 2), w - w // 2):
            t = ext_ref[pl.ds(HALO + d, tm), lanes]
            s = t if s is None else s + t
        lo = jnp.maximum(pos - w // 2, 0)
        hi = jnp.minimum(pos + (w - 1 - w // 2), L - 1)
        cnt = (hi - lo + 1).astype(F32)
        pooled = s / cnt - ext_ref[pl.ds(HALO, tm), lanes]
        mixed = jnp.dot(pooled.astype(BF16), w_ref[g], preferred_element_type=F32)
        o_ref[:, lanes] = mixed * sc_ref[:, lanes]


def _conf_prepare(ext_ref, hs_ref, tm):
    a = ext_ref[:, pl.ds(0, BRANCH_W)]
    gt = ext_ref[:, pl.ds(BRANCH_W, BRANCH_W)]
    hs_ref[0] = a * jax.nn.sigmoid(gt)
    shifted_rows = tm + 2 * HALO - SUBLANES
    for b in range(1, SUBLANES):
        hs_ref[b, pl.ds(0, shifted_rows), :] = hs_ref[0, pl.ds(b, shifted_rows), :]


def _conf_taps(hs_ref, w_ref, b_ref, o_ref, row_blocks):
    grp = (CONF_ROWS // SUBLANES, SUBLANES, BRANCH_W)
    for rb in row_blocks:
        r0 = rb * CONF_ROWS
        acc = jnp.zeros(grp, F32)
        for j in range(CF_WIDTH):
            a8, b = divmod(HALO - CF_WIDTH // 2 + j, SUBLANES)
            acc = acc + w_ref[j] * hs_ref[b, pl.ds(r0 + SUBLANES * a8, CONF_ROWS), :].reshape(grp)
        o_ref[pl.ds(r0, CONF_ROWS), :] = acc.reshape(CONF_ROWS, BRANCH_W) + b_ref[...]


def _conf_finish(o_ref, g_ref, beta_ref):
    acc = o_ref[...]
    mu = jnp.mean(acc, axis=-1, keepdims=True)
    cen = acc - mu
    var = jnp.mean(cen * cen, axis=-1, keepdims=True)
    y = cen * lax.rsqrt(var + EPS) * g_ref[...] + beta_ref[...]
    o_ref[...] = y * jax.nn.sigmoid(y)


def _rel_bucket_np(rel):
    nb = REL_BUCKETS // 2
    max_exact = nb // 2
    ret = np.where(rel > 0, nb, 0)
    n = np.abs(rel)
    ratio = np.log(np.maximum(n, 1).astype(np.float32) / np.float32(max_exact)) / np.float32(
        math.log(REL_MAX_DIST / max_exact))
    large = max_exact + (ratio * np.float32(nb - max_exact)).astype(np.int32)
    large = np.minimum(large, nb - 1)
    return ret + np.where(n < max_exact, n, large)


def _bias_buckets():
    r = np.arange(ATT_TK)[:, None]
    c = np.arange(ATT_TQ)[None, :]
    span = ATT_NEAR - 1
    out = np.zeros((ATT_TK // ATT_TQ, 2 * span + 1, ATT_TK, ATT_TQ), np.int32)
    for e in range(ATT_TK // ATT_TQ):
        for dj in range(-span, span + 1):
            out[e, dj + span] = _rel_bucket_np(ATT_TK * dj + r - ATT_TQ * e - c)
        assert (out[e, :span - 1] == REL_BUCKETS // 2 - 1).all() and (out[e, span + 2:] == REL_BUCKETS - 1).all()
    return out


def _attn_kernel(cfar_ref, q_ref, k_ref, vt_ref, bias_ref, lq1_ref, lk1_ref, lq2_ref, lk2_ref, sg_ref,
                 o_ref, qt_ref, s_ref, smax_ref, m_ref, acc_ref, *, nk, lam_init):
    h = pl.program_id(1)
    qi = pl.program_id(2)
    qt = q_ref[...].astype(F32).T
    feat = lax.broadcasted_iota(jnp.int32, qt.shape, 0)
    qt_ref[0] = jnp.where(feat < DA_HEAD_DIM, qt, 0.0).astype(BF16)
    qt_ref[1] = jnp.where(feat >= DA_HEAD_DIM, qt, 0.0).astype(BF16)
    m_ref[...] = jnp.full(m_ref.shape, NEG_BIG, F32)
    acc_ref[...] = jnp.zeros(acc_ref.shape, F32)
    j0 = qi // (ATT_TK // ATT_TQ)
    e = qi % (ATT_TK // ATT_TQ)
    c_left = cfar_ref[h, 0]
    c_right = cfar_ref[h, 1]

    def scores(j, slot, near_kind=None):
        kblk = k_ref[pl.ds(pl.multiple_of(j * ATT_TK, ATT_TK), ATT_TK), :]
        for mm in range(2):
            s = jnp.dot(kblk, qt_ref[mm], preferred_element_type=F32)
            if near_kind is not None:
                s = s + bias_ref[e, near_kind]
            s_ref[slot, mm] = s
            smax_ref[slot, mm] = jnp.max(s, axis=0, keepdims=True)

    def softmax_values(j, slot, off):
        vblk = vt_ref[j]
        for mm in range(2):
            m_prev = m_ref[mm]
            m_new = jnp.maximum(m_prev, smax_ref[slot, mm] + off)
            p = jnp.exp2(s_ref[slot, mm] - (m_new - off)).astype(BF16)
            acc_ref[mm] = (jnp.exp2(m_prev - m_new) * acc_ref[mm]
                           + jnp.dot(vblk, p, preferred_element_type=F32))
            m_ref[mm] = m_new

    near_lo = jnp.clip(j0 - 1, 0, nk - ATT_NEAR)
    n_far = nk - ATT_NEAR

    def far_chunk(t):
        t = jnp.minimum(t, n_far - 1)
        return jnp.where(t < near_lo, t, t + ATT_NEAR)

    def far_offset(j):
        return jnp.where(j < j0, c_left, c_right)

    scores(near_lo, 0, near_lo - j0 + ATT_NEAR - 1)
    for i in range(ATT_NEAR):
        j = near_lo + i
        if i + 1 < ATT_NEAR:
            scores(j + 1, (i + 1) % 2, j + 1 - j0 + ATT_NEAR - 1)
        else:
            scores(far_chunk(0), (i + 1) % 2)
        softmax_values(j, i % 2, 0.0)

    def body(i, carry):
        for u in range(ATT_UNROLL):
            t = ATT_UNROLL * i + u
            cur = u % 2
            scores(far_chunk(t + 1), 1 - cur)
            j = far_chunk(t)
            softmax_values(j, cur, far_offset(j))
        return carry

    lax.fori_loop(0, n_far // ATT_UNROLL, body, 0)

    lam = (jnp.exp(jnp.sum(lq1_ref[...] * lk1_ref[...], keepdims=True))
           - jnp.exp(jnp.sum(lq2_ref[...] * lk2_ref[...], keepdims=True)) + lam_init)
    a0 = acc_ref[0]
    a1 = acc_ref[1]
    o = (a0[:DA_V_DIM] / a0[DA_V_DIM:DA_V_DIM + 1]
         - lam * (a1[:DA_V_DIM] / a1[DA_V_DIM:DA_V_DIM + 1]))
    ms = jnp.mean(o * o, axis=0, keepdims=True)
    y = o * lax.rsqrt(ms + EPS) * sg_ref[...] * (1.0 - lam_init)
    o_ref[...] = y.T


def _attention(qk, vt, bias, cfar, lq1, lk1, lq2, lk2, sg, lam_init):
    B, L, _ = qk.shape
    nk = L // ATT_TK
    assert (nk - ATT_NEAR) % ATT_UNROLL == 0 and ATT_UNROLL % 2 == 0 and ATT_NEAR % 2 == 0 and nk > ATT_NEAR
    vec = pl.BlockSpec((1, DA_HEAD_DIM), lambda b, h, i: (0, 0))
    return pl.pallas_call(
        functools.partial(_attn_kernel, nk=nk, lam_init=lam_init),
        grid=(B, DA_HEADS, L // ATT_TQ),
        in_specs=[pl.BlockSpec(memory_space=pltpu.SMEM),
                  pl.BlockSpec((None, ATT_TQ, LANES), lambda b, h, i: (b, i, h)),
                  pl.BlockSpec((None, L, LANES), lambda b, h, i: (b, 0, DA_HEADS + h)),
                  pl.BlockSpec((None, None, nk, V_ROWS, ATT_TK), lambda b, h, i: (b, h, 0, 0, 0)),
                  pl.BlockSpec((None, ATT_TK // ATT_TQ, 2 * ATT_NEAR - 1, ATT_TK, ATT_TQ), lambda b, h, i: (h, 0, 0, 0, 0)),
                  vec, vec, vec, vec,
                  pl.BlockSpec((DA_V_DIM, 1), lambda b, h, i: (0, 0))],
        out_specs=pl.BlockSpec((None, ATT_TQ, LANES), lambda b, h, i: (b, i, h)),
        out_shape=jax.ShapeDtypeStruct((B, L, BRANCH_W), F32),
        scratch_shapes=[pltpu.VMEM((2, LANES, ATT_TQ), BF16),
                        pltpu.VMEM((2, 2, ATT_TK, ATT_TQ), F32),
                        pltpu.VMEM((2, 2, 1, ATT_TQ), F32),
                        pltpu.VMEM((2, 1, ATT_TQ), F32),
                        pltpu.VMEM((2, V_ROWS, ATT_TQ), F32)],
        compiler_params=_cparams(("parallel", "parallel", "arbitrary"), 52),
        name="diff_attention",
    )(cfar, qk, qk, vt, bias, lq1, lk1, lq2, lk2, sg)


def _spectrum_rows(n2):
    return -(-(n2 // 2 + 1) // (2 * SUBLANES)) * (2 * SUBLANES)


def _fft_consts(n2):
    n1 = FFT_N1
    n = n1 * n2
    f2 = np.exp(-2j * np.pi * np.outer(np.arange(n2), np.arange(n2)) / n2)
    tw = np.exp(-2j * np.pi * np.outer(np.arange(n2), np.arange(n1)) / n)
    f1 = np.exp(-2j * np.pi * np.outer(np.arange(n1), np.arange(n1)) / n1)
    nr = n2 // 2 + 1
    pad = ((0, _spectrum_rows(n2) - nr), (0, 0))
    c = {}
    c["f2_full"] = np.concatenate([np.pad(f2.real[:nr], pad), np.pad(f2.imag[:nr], pad)], 0)
    c["f2_half"] = c["f2_full"][:, : n2 // 2]
    c["tre"], c["tim"] = np.pad(tw.real[:nr], pad), np.pad(tw.imag[:nr], pad)
    c["g_top"] = np.concatenate([f1.real, f1.imag], 1)
    c["g_bot"] = np.concatenate([-f1.imag, f1.real], 1)
    c["gi_top"] = np.concatenate([f1.real, -f1.imag], 1)
    c["gi_bot"] = np.concatenate([f1.imag, f1.real], 1)
    wgt = np.full((nr, 1), 2.0)
    wgt[0] = wgt[nr - 1] = 1.0
    c["fin_re"] = np.pad(wgt * f2.real[:nr], pad).T[: n2 // 2]
    c["fin_im"] = np.pad(wgt * f2.imag[:nr], pad).T[: n2 // 2]
    out = {}
    for k, v in c.items():
        out[k] = jnp.asarray(v, F32 if k in ("tre", "tim") else BF16)
    return out


def _filter_td_kernel(emb_ref, t_ref, w1_ref, b1_ref, f0_ref, w2_ref, b2_ref, f1_ref, w3_ref, ad_ref, o_ref):
    emb = emb_ref[...].astype(BF16)
    h = jnp.dot(w1_ref[...], emb, preferred_element_type=F32) + b1_ref[...]
    h = jnp.sin(f0_ref[...] * h)
    h = jnp.dot(w2_ref[...], h.astype(BF16), preferred_element_type=F32) + b2_ref[...]
    h = jnp.sin(f1_ref[...] * h)
    h = jnp.dot(w3_ref[...], h.astype(BF16), preferred_element_type=F32)
    decay = jnp.exp(-(ad_ref[...] * t_ref[...]))
    o_ref[...] = h * decay


def _filter_td(emb_t, t_full, w1t, b1, f0, w2t, b2, f1, w3t, absdelta):
    n = emb_t.shape[1]
    pt = 2048
    nch = n // pt
    col = lambda r: pl.BlockSpec((r, 1), lambda i: (0, 0))
    return pl.pallas_call(
        _filter_td_kernel,
        grid=(nch,),
        in_specs=[pl.BlockSpec((HY_EMB_PAD, pt), lambda i: (0, i)),
                  pl.BlockSpec((1, pt), lambda i: (0, i)),
                  pl.BlockSpec((HY_ORDER, HY_EMB_PAD), lambda i: (0, 0)), col(HY_ORDER), col(HY_ORDER),
                  pl.BlockSpec((HY_ORDER, HY_ORDER), lambda i: (0, 0)), col(HY_ORDER), col(HY_ORDER),
                  pl.BlockSpec((None, BRANCH_W, HY_ORDER), lambda i: (i // (nch // 2), 0, 0)),
                  col(BRANCH_W)],
        out_specs=pl.BlockSpec((BRANCH_W, pt), lambda i: (0, i)),
        out_shape=jax.ShapeDtypeStruct((BRANCH_W, n), F32),
        compiler_params=_cparams(("parallel",), 32),
        name="hyena_filter_time",
    )(emb_t, t_full, w1t, b1, f0, w2t, b2, f1, w3t, absdelta)


def _cmul(are, aim, bre, bim):
    return are * bre - aim * bim, are * bim + aim * bre


def _filter_fft_kernel(f_ref, f2_ref, tre_ref, tim_ref, gt_ref, gb_ref, o_ref, bre_ref, bim_ref, *, cc, n2):
    n_total = FFT_N1 * n2
    nr = _spectrum_rows(n2)
    row = lax.broadcasted_iota(jnp.int32, (n2, FFT_N1), 0)
    lane = lax.broadcasted_iota(jnp.int32, (n2, FFT_N1), 1)
    is_lag0_backward = (row == n2 // 2) & (lane == 0)

    def per_channel(c, carry):
        f = f_ref[c]
        s = jnp.sum(jnp.abs(f), keepdims=True)
        f = jnp.where(is_lag0_backward, 0.0, f) * (1.0 / ((s + EPS) * n_total))
        a = jnp.dot(f2_ref[...], f.astype(BF16), preferred_element_type=F32)
        bre, bim = _cmul(a[:nr], a[nr:], tre_ref[...], tim_ref[...])
        bre_ref[c] = bre.astype(BF16)
        bim_ref[c] = bim.astype(BF16)
        return carry

    lax.fori_loop(0, cc, per_channel, 0, unroll=HY_CH_UNROLL)
    bre = bre_ref[...].reshape(cc * nr, FFT_N1)
    bim = bim_ref[...].reshape(cc * nr, FFT_N1)
    o_ref[...] = (jnp.dot(bre, gt_ref[...], preferred_element_type=F32)
                  + jnp.dot(bim, gb_ref[...], preferred_element_type=F32))


def _filter_fft(filt, consts, n2, cc):
    C = filt.shape[0]
    nr = _spectrum_rows(n2)
    full = lambda a: pl.BlockSpec(a.shape, lambda i: (0,) * a.ndim)
    mats = [consts["f2_full"], consts["tre"], consts["tim"], consts["g_top"], consts["g_bot"]]
    return pl.pallas_call(
        functools.partial(_filter_fft_kernel, cc=cc, n2=n2),
        grid=(C // cc,),
        in_specs=[pl.BlockSpec((cc, n2, FFT_N1), lambda i: (i, 0, 0))] + [full(a) for a in mats],
        out_specs=pl.BlockSpec((cc * nr, 2 * FFT_N1), lambda i: (i, 0)),
        out_shape=jax.ShapeDtypeStruct((C * nr, 2 * FFT_N1), F32),
        scratch_shapes=[pltpu.VMEM((cc, nr, FFT_N1), BF16), pltpu.VMEM((cc, nr, FFT_N1), BF16)],
        compiler_params=_cparams(("parallel",), 40),
        name="hyena_filter_fft",
    )(filt, *mats)


def _shift_prev(x, row, lane):
    a = pltpu.roll(x, 1, axis=1)
    b = pltpu.roll(a, 1, axis=0)
    return jnp.where(lane == 0, jnp.where(row == 0, 0.0, b), a)


def _shift_next(x, row, lane, rows):
    a = pltpu.roll(x, FFT_N1 - 1, axis=1)
    b = pltpu.roll(a, rows - 1, axis=0)
    return jnp.where(lane == FFT_N1 - 1, jnp.where(row == rows - 1, 0.0, b), a)


def _hyena_kernel(sw_ref, sb_ref, d_ref, x0_ref, x1_ref, v_ref, hf_ref, f2_ref, tre_ref, tim_ref,
                  gt_ref, gb_ref, git_ref, gib_ref, fre_ref, fim_ref, o_ref,
                  z_ref, x0c_ref, bre_ref, bim_ref, c_ref, *, cc, n2):
    rows = n2 // 2
    nr = _spectrum_rows(n2)
    ch0 = pl.program_id(1) * cc
    row = lax.broadcasted_iota(jnp.int32, (rows, FFT_N1), 0)
    lane = lax.broadcasted_iota(jnp.int32, (rows, FFT_N1), 1)

    def short_conv(x, ch):
        return (sw_ref[0, ch] * _shift_prev(x, row, lane) + sw_ref[1, ch] * x
                + sw_ref[2, ch] * _shift_next(x, row, lane, rows) + sb_ref[ch])

    def forward(c, carry):
        ch = ch0 + c
        x0c_ref[c] = short_conv(x0_ref[c], ch)
        z = short_conv(x1_ref[c], BRANCH_W + ch) * short_conv(v_ref[c], 2 * BRANCH_W + ch)
        z_ref[c] = z
        a = jnp.dot(f2_ref[...], z.astype(BF16), preferred_element_type=F32)
        bre, bim = _cmul(a[:nr], a[nr:], tre_ref[...], tim_ref[...])
        bre_ref[c] = bre.astype(BF16)
        bim_ref[c] = bim.astype(BF16)
        return carry

    lax.fori_loop(0, cc, forward, 0, unroll=HY_CH_UNROLL)

    bre = bre_ref[...].reshape(cc * nr, FFT_N1)
    bim = bim_ref[...].reshape(cc * nr, FFT_N1)
    x = (jnp.dot(bre, gt_ref[...], preferred_element_type=F32)
         + jnp.dot(bim, gb_ref[...], preferred_element_type=F32))
    hf = hf_ref[...]
    yre, yim = _cmul(x[:, :FFT_N1], x[:, FFT_N1:], hf[:, :FFT_N1], hf[:, FFT_N1:])
    cfull = (jnp.dot(yre.astype(BF16), git_ref[...], preferred_element_type=F32)
             + jnp.dot(yim.astype(BF16), gib_ref[...], preferred_element_type=F32))
    c_ref[...] = cfull.reshape(cc, nr, 2 * FFT_N1)

    def backward(c, carry):
        cm = c_ref[c]
        cre, cim = cm[:, :FFT_N1], cm[:, FFT_N1:]
        tre, tim = tre_ref[...], tim_ref[...]
        dre = cre * tre + cim * tim
        dim = cim * tre - cre * tim
        y = (jnp.dot(fre_ref[...], dre.astype(BF16), preferred_element_type=F32)
             + jnp.dot(fim_ref[...], dim.astype(BF16), preferred_element_type=F32))
        o_ref[c] = x0c_ref[c] * (y + z_ref[c] * d_ref[ch0 + c])
        return carry

    lax.fori_loop(0, cc, backward, 0, unroll=HY_CH_UNROLL)


def _hyena(u, hf, consts, short_w, short_b, d_bias, n2, cc):
    B = u.shape[0]
    rows = n2 // 2
    nr = _spectrum_rows(n2)
    C = BRANCH_W
    nblk = C // cc
    smem = pl.BlockSpec(memory_space=pltpu.SMEM)
    full = lambda a: pl.BlockSpec(a.shape, lambda b, i: (0,) * a.ndim)
    mats = [consts[k] for k in ("f2_half", "tre", "tim", "g_top", "g_bot", "gi_top", "gi_bot", "fin_re", "fin_im")]
    sec = lambda s: pl.BlockSpec((None, cc, rows, FFT_N1), lambda b, i: (b, s * nblk + i, 0, 0))
    return pl.pallas_call(
        functools.partial(_hyena_kernel, cc=cc, n2=n2),
        grid=(B, nblk),
        in_specs=[smem, smem, smem, sec(0), sec(1), sec(2),
                  pl.BlockSpec((cc * nr, 2 * FFT_N1), lambda b, i: (i, 0))] + [full(a) for a in mats],
        out_specs=pl.BlockSpec((None, cc, rows, FFT_N1), lambda b, i: (b, i, 0, 0)),
        out_shape=jax.ShapeDtypeStruct((B, C, rows, FFT_N1), F32),
        scratch_shapes=[pltpu.VMEM((cc, rows, FFT_N1), F32), pltpu.VMEM((cc, rows, FFT_N1), F32),
                        pltpu.VMEM((cc, nr, FFT_N1), BF16), pltpu.VMEM((cc, nr, FFT_N1), BF16),
                        pltpu.VMEM((cc, nr, 2 * FFT_N1), F32)],
        compiler_params=_cparams(("parallel", "arbitrary"), 40),
        name="hyena_mixer",
    )(short_w, short_b, d_bias, u, u, u, hf, *mats)


def _hyena_tables(L):
    t = jnp.linspace(0.0, 1.0, L, dtype=F32)[:, None]
    w = 2.0 * math.pi * jnp.arange(L, dtype=F32)[:, None] / L
    bands = jnp.linspace(1e-4, HY_BANDS - 1, HY_BANDS, dtype=F32)[None, :]
    emb = jnp.concatenate([t, jnp.cos(bands * w), -jnp.sin(bands * w)], axis=-1)
    two_sided = lambda a: jnp.concatenate([a, a[:1], a[:0:-1]], axis=0)
    emb_full = jnp.pad(two_sided(emb), ((0, 0), (0, HY_EMB_PAD - HY_EMB)))
    t_full = two_sided(t).T
    return emb_full.T, t_full


def _merge_kernel(x_ref, pa_ref, pa_prev_ref, pa_next_ref, yb_ref, yct_ref, cf_ref, cf_prev_ref, cf_next_ref,
                  g_ref, ws_ref, wm_ref, wb_ref, wo_ref, fg_ref, pw_ref, psc_ref, dw_ref, db_ref, lg_ref, lb_ref,
                  o_ref, pext_ref, cext_ref, hs_ref, ya_ref, yd_ref, *, final, tm, L):
    _fill_ext(pext_ref, pa_ref, pa_prev_ref, pa_next_ref, tm, L)
    _fill_ext(cext_ref, cf_ref, cf_prev_ref, cf_next_ref, tm, L)
    _pool_body(pext_ref, pw_ref, psc_ref, ya_ref, tm, L)
    _conf_prepare(cext_ref, hs_ref, tm)
    x = x_ref[...]
    ms = jnp.mean(x * x, axis=-1, keepdims=True)
    h = (x * lax.rsqrt(ms + EPS) * g_ref[...]).astype(BF16)
    mixed = None
    n_blocks = tm // CONF_ROWS
    for step, n in enumerate((1, 2, 0, 3)):
        if step < 3:
            _conf_taps(hs_ref, dw_ref, db_ref, yd_ref, range(step * n_blocks // 3, (step + 1) * n_blocks // 3))
        else:
            _conf_finish(yd_ref, lg_ref, lb_ref)
        if n == 0:
            y = ya_ref[...]
        elif n == 1:
            y = yb_ref[...]
        elif n == 2:
            y = yct_ref[...].T
        else:
            y = yd_ref[...]
        sg = jnp.dot(h, ws_ref[:, n * BRANCH_W:(n + 1) * BRANCH_W], preferred_element_type=F32)
        u = (y * (sg * jax.nn.sigmoid(sg))).astype(BF16)
        proj = jnp.dot(u, wb_ref[n], preferred_element_type=F32)
        merge = jax.nn.sigmoid(jnp.dot(h, wm_ref[:, n * D_MODEL:(n + 1) * D_MODEL], preferred_element_type=F32))
        mixed = merge * proj if mixed is None else mixed + merge * proj
    out = x + jnp.dot(mixed.astype(BF16), wo_ref[...], preferred_element_type=F32)
    if final:
        ms = jnp.mean(out * out, axis=-1, keepdims=True)
        out = out * lax.rsqrt(ms + EPS) * fg_ref[...]
    o_ref[...] = out


def _merge(x2, pool_in, yb, yct, cf, p, fg, L, final):
    T, D = x2.shape
    tm = MERGE_TILE
    nl = L // tm
    W = BRANCH_W
    tok = lambda c: pl.BlockSpec((tm, c), lambda i: (i, 0))
    const = lambda a: pl.BlockSpec(a.shape, lambda i: (0,) * a.ndim, pipeline_mode=pl.Buffered(1))
    consts = [p["g"], p["w_silu"], p["w_merge"], p["w_branch"], p["w_out"], fg, p["pool_w"], p["pool_scale"],
              p["dw_w"], p["dw_b"], p["ln_g"], p["ln_b"]]
    return pl.pallas_call(
        functools.partial(_merge_kernel, final=final, tm=tm, L=L),
        grid=(T // tm,),
        in_specs=[tok(D)] + _halo_specs(tm, W, T) + [tok(W), pl.BlockSpec((None, W, tm), lambda i: (i // nl, 0, i % nl))]
        + _halo_specs(tm, 2 * W, T) + [const(a) for a in consts],
        out_specs=tok(D),
        out_shape=jax.ShapeDtypeStruct((T, D), F32),
        scratch_shapes=[pltpu.VMEM((tm + 2 * HALO, W), F32), pltpu.VMEM((tm + 2 * HALO, 2 * W), F32),
                        pltpu.VMEM((SUBLANES, tm + 2 * HALO, W), F32), pltpu.VMEM((tm, W), F32),
                        pltpu.VMEM((tm, W), F32)],
        compiler_params=_cparams(("parallel",), 56),
        name="gated_merge",
    )(x2, pool_in, pool_in, pool_in, yb, yct, cf, cf, cf, *consts)


def _layer_weights(l, norm_g, w_in, pool_w, pool_scale, subln_g, rel_bias, hy_short_w, hy_short_b, hy_w1, hy_b1,
                   hy_freq, hy_w2, hy_b2, hy_w3, hy_d, cf_dw_w, cf_dw_b, cf_ln_g, cf_ln_b, w_branch, w_out):
    w = w_in[l].astype(BF16)
    W = BRANCH_W
    p = {}
    p["g"] = norm_g[l][None, :]
    p["w_tok"] = jnp.concatenate([w[:, OFF_POOL:OFF_V], w[:, OFF_CF:OFF_SILU]], axis=1)
    p["w_chan"] = w[:, OFF_V:OFF_CF].T
    p["w_silu"] = w[:, OFF_SILU:OFF_MERGE]
    p["w_merge"] = w[:, OFF_MERGE:]
    p["w_branch"] = w_branch[l].astype(BF16)
    p["w_out"] = w_out[l].astype(BF16)
    p["pool_w"] = pool_w[l].astype(BF16)
    p["pool_scale"] = pool_scale[l][None, :]
    p["subln_g"] = subln_g[l][:, None]
    buckets = jnp.asarray(_bias_buckets())[None]
    table_t = rel_bias.astype(F32).T * LOG2E
    bias = jnp.zeros((DA_HEADS,) + buckets.shape[1:], F32)
    for b in range(REL_BUCKETS):
        bias = jnp.where(buckets == b, table_t[:, b][:, None, None, None, None], bias)
    p["bias"] = bias
    p["cfar"] = jnp.stack([table_t[:, REL_BUCKETS // 2 - 1], table_t[:, REL_BUCKETS - 1]], axis=1)
    p["short_w"] = hy_short_w[l]
    p["short_b"] = hy_short_b[l]
    p["hy_d"] = hy_d[l]
    p["w1t"] = jnp.pad(hy_w1[l].T, ((0, 0), (0, HY_EMB_PAD - HY_EMB))).astype(BF16)
    p["b1"] = hy_b1[l][:, None]
    p["f0"] = hy_freq[l, 0][:, None]
    p["w2t"] = hy_w2[l].T.astype(BF16)
    p["b2"] = hy_b2[l][:, None]
    p["f1"] = hy_freq[l, 1][:, None]
    p["w3t"] = hy_w3[l].T.reshape(2, W, HY_ORDER).astype(BF16)
    max_decay = math.log(HY_DECAY_TARGET) / HY_FAST
    min_decay = math.log(HY_DECAY_TARGET) / HY_SLOW
    p["absdelta"] = jnp.abs(jnp.linspace(min_decay, max_decay, W, dtype=F32))[:, None]
    p["dw_w"] = jnp.broadcast_to(cf_dw_w[l][:, None, :], (CF_WIDTH, SUBLANES, BRANCH_W))
    p["dw_b"] = cf_dw_b[l][None, :]
    p["ln_g"] = cf_ln_g[l][None, :]
    p["ln_b"] = cf_ln_b[l][None, :]
    return p


def _hyena_chunk(n2):
    return max(1, 1024 // n2)


def _filter_spectrum(p, L):
    n2 = 2 * L // FFT_N1
    consts = _fft_consts(n2)
    emb_t, t_full = _hyena_tables(L)
    filt = _filter_td(emb_t, t_full, p["w1t"], p["b1"], p["f0"], p["w2t"], p["b2"], p["f1"], p["w3t"],
                      p["absdelta"])
    hf = _filter_fft(filt.reshape(BRANCH_W, n2, FFT_N1), consts, n2, _hyena_chunk(n2))
    return hf, consts, n2


def _layer(x, p, l, lq1, lk1, lq2, lk2, final_g, final):
    B, L, D = x.shape
    T = B * L
    x2 = x.reshape(T, D)
    g = p["g"]
    lam_init = 0.8 - 0.6 * math.exp(-0.3 * l)

    pool_in, qk, cf, vt, hyt = _project(x2, g, p["w_tok"], p["w_chan"], B, L)
    qk = qk.reshape(B, L, 2 * BRANCH_W)

    yb = _attention(qk, vt, p["bias"], p["cfar"], lq1[l][None, :], lk1[l][None, :], lq2[l][None, :],
                    lk2[l][None, :], p["subln_g"], lam_init).reshape(T, BRANCH_W)
    hf, consts, n2 = _filter_spectrum(p, L)
    yct = _hyena(hyt.reshape(B, 3 * BRANCH_W, n2 // 2, FFT_N1), hf, consts, p["short_w"], p["short_b"], p["hy_d"],
                 n2, _hyena_chunk(n2)).reshape(B, BRANCH_W, L)
    out = _merge(x2, pool_in, yb, yct, cf, p, final_g[None, :], L, final)
    return out.reshape(B, L, D)


def kernel(x_prompt, x_sample, norm_g, w_in, pool_w, pool_scale, lambda_q1, lambda_k1, lambda_q2, lambda_k2, subln_g, rel_bias, hy_short_w, hy_short_b, hy_w1, hy_b1, hy_freq, hy_w2, hy_b2, hy_w3, hy_d, cf_dw_w, cf_dw_b, cf_ln_g, cf_ln_b, w_branch, w_out, final_g):
    depth = norm_g.shape[0]
    hp, hs = x_prompt, x_sample
    for l in range(depth):
        p = _layer_weights(l, norm_g, w_in, pool_w, pool_scale, subln_g, rel_bias, hy_short_w, hy_short_b, hy_w1,
                           hy_b1, hy_freq, hy_w2, hy_b2, hy_w3, hy_d, cf_dw_w, cf_dw_b, cf_ln_g, cf_ln_b, w_branch,
                           w_out)
        final = l == depth - 1
        hp = _layer(hp, p, l, lambda_q1, lambda_k1, lambda_q2, lambda_k2, final_g, final)
        hs = _layer(hs, p, l, lambda_q1, lambda_k1, lambda_q2, lambda_k2, final_g, final)
    return (hp, hs)
```

```python
import functools
import math

import numpy as np
import jax
import jax.numpy as jnp
from jax import lax
from jax.experimental import pallas as pl
from jax.experimental.pallas import tpu as pltpu

F32 = jnp.float32
BF16 = jnp.bfloat16

D_MODEL = 1024
BRANCH_W = 512
N_BRANCH = 4
POOL_WINDOWS = (2, 4, 8, 16)
POOL_GW = 128
DA_HEADS = 4
DA_HEAD_DIM = 64
DA_V_DIM = 128
REL_BUCKETS = 32
REL_MAX_DIST = 128
HY_EMB = 33
HY_EMB_PAD = 40
HY_BANDS = 16
HY_ORDER = 64
HY_DECAY_TARGET = 1e-2
HY_FAST = 0.3
HY_SLOW = 1.5
CF_WIDTH = 31
EPS = 1e-6

OFF_POOL = 0
OFF_Q = 512
OFF_K = 1024
OFF_V = 1536
OFF_HY = 2048
OFF_CF = 3584
OFF_SILU = 4608
OFF_MERGE = 6656

LANES = 128
SUBLANES = 8
CONF_ROWS = 32
FFT_N1 = 256
HALO = 16
MERGE_TILE = 256
ATT_TQ = 512
ATT_TK = 512
ATT_UNROLL = 4
ATT_NEAR = 4
HY_CH_UNROLL = 16
NEG_BIG = -1e30
LOG2E = math.log2(math.e)
V_ROWS = DA_V_DIM + 16


def _cparams(sem, vmem_mb, flags=None):
    return pltpu.CompilerParams(dimension_semantics=sem, vmem_limit_bytes=vmem_mb << 20, flags=flags)


def _proj_kernel(x_ref, g_ref, wtok_ref, wchan_ref, pool_ref, qk_ref, cf_ref, vt_ref, hyt_ref):
    x = x_ref[...]
    ms = jnp.mean(x * x, axis=-1, keepdims=True)
    h = (x * lax.rsqrt(ms + EPS) * g_ref[...]).astype(BF16)
    W = BRANCH_W

    def tok(lo, hi):
        return jnp.dot(h, wtok_ref[:, lo:hi], preferred_element_type=F32)

    def chan(lo, hi):
        return lax.dot_general(wchan_ref[lo:hi, :], h, (((1,), (1,)), ((), ())), preferred_element_type=F32)

    pool_ref[...] = tok(0, W)
    qk_ref[:, pl.ds(0, W)] = (tok(W, 2 * W) * (LOG2E * DA_HEAD_DIM ** -0.5)).astype(BF16)
    qk_ref[:, pl.ds(W, W)] = tok(2 * W, 3 * W).astype(BF16)
    cf_ref[...] = tok(3 * W, 5 * W)
    for hd in range(DA_HEADS):
        vt_ref[hd, 0, pl.ds(0, DA_V_DIM), :] = chan(hd * DA_V_DIM, (hd + 1) * DA_V_DIM).astype(BF16)
        vt_ref[hd, 0, pl.ds(DA_V_DIM, V_ROWS - DA_V_DIM), :] = jnp.ones((V_ROWS - DA_V_DIM, x.shape[0]), BF16)
    for s in range(3):
        hyt_ref[pl.ds(s * W, W), :] = chan((1 + s) * W, (2 + s) * W)


def _project(x2, g, wtok, wchan, B, L):
    T, D = x2.shape
    tm = ATT_TK
    nl = L // tm
    W = BRANCH_W
    const = lambda a: pl.BlockSpec(a.shape, lambda i: (0,) * a.ndim, pipeline_mode=pl.Buffered(1))
    tok = lambda c: pl.BlockSpec((tm, c), lambda i: (i, 0))
    return pl.pallas_call(
        _proj_kernel,
        grid=(T // tm,),
        in_specs=[tok(D), const(g), const(wtok), const(wchan)],
        out_specs=[tok(W), tok(2 * W), tok(2 * W),
                   pl.BlockSpec((None, DA_HEADS, 1, V_ROWS, tm), lambda i: (i // nl, 0, i % nl, 0, 0)),
                   pl.BlockSpec((None, 3 * W, tm), lambda i: (i // nl, 0, i % nl))],
        out_shape=[jax.ShapeDtypeStruct((T, W), F32), jax.ShapeDtypeStruct((T, 2 * W), BF16),
                   jax.ShapeDtypeStruct((T, 2 * W), F32),
                   jax.ShapeDtypeStruct((B, DA_HEADS, nl, V_ROWS, tm), BF16),
                   jax.ShapeDtypeStruct((B, 3 * W, L), F32)],
        compiler_params=_cparams(("parallel",), 48),
        name="input_projection",
    )(x2, g, wtok, wchan)


def _halo_specs(tm, C, T):
    r = tm // HALO
    last = T // HALO - 1
    return [pl.BlockSpec((tm, C), lambda i: (i, 0)),
            pl.BlockSpec((HALO, C), lambda i: (jnp.maximum(i * r - 1, 0), 0)),
            pl.BlockSpec((HALO, C), lambda i: (jnp.minimum((i + 1) * r, last), 0))]


def _fill_ext(ext_ref, cur_ref, prev_ref, next_ref, tm, L):
    i = pl.program_id(0)
    tiles_per_seq = L // tm
    first = (i % tiles_per_seq) == 0
    lastt = (i % tiles_per_seq) == tiles_per_seq - 1
    ext_ref[pl.ds(0, HALO), :] = jnp.where(first, 0.0, prev_ref[...])
    ext_ref[pl.ds(HALO, tm), :] = cur_ref[...]
    ext_ref[pl.ds(HALO + tm, HALO), :] = jnp.where(lastt, 0.0, next_ref[...])


def _pool_body(ext_ref, w_ref, sc_ref, o_ref, tm, L):
    i = pl.program_id(0)
    pos = (i % (L // tm)) * tm + lax.broadcasted_iota(jnp.int32, (tm, 1), 0)
    for g, w in enumerate(POOL_WINDOWS):
        lanes = pl.ds(g * POOL_GW, POOL_GW)
        s = None
        for d in range(-(w // 2), w - w // 2):
            t = ext_ref[pl.ds(HALO + d, tm), lanes]
            s = t if s is None else s + t
        lo = jnp.maximum(pos - w // 2, 0)
        hi = jnp.minimum(pos + (w - 1 - w // 2), L - 1)
        cnt = (hi - lo + 1).astype(F32)
        pooled = s / cnt - ext_ref[pl.ds(HALO, tm), lanes]
        mixed = jnp.dot(pooled.astype(BF16), w_ref[g], preferred_element_type=F32)
        o_ref[:, lanes] = mixed * sc_ref[:, lanes]


def _conf_prepare(ext_ref, hs_ref, tm):
    a = ext_ref[:, pl.ds(0, BRANCH_W)]
    gt = ext_ref[:, pl.ds(BRANCH_W, BRANCH_W)]
    hs_ref[0] = a * jax.nn.sigmoid(gt)
    shifted_rows = tm + 2 * HALO - SUBLANES
    for b in range(1, SUBLANES):
        hs_ref[b, pl.ds(0, shifted_rows), :] = hs_ref[0, pl.ds(b, shifted_rows), :]


def _conf_taps(hs_ref, w_ref, b_ref, o_ref, row_blocks):
    grp = (CONF_ROWS // SUBLANES, SUBLANES, BRANCH_W)
    for rb in row_blocks:
        r0 = rb * CONF_ROWS
        acc = jnp.zeros(grp, F32)
        for j in range(CF_WIDTH):
            a8, b = divmod(HALO - CF_WIDTH // 2 + j, SUBLANES)
            acc = acc + w_ref[j] * hs_ref[b, pl.ds(r0 + SUBLANES * a8, CONF_ROWS), :].reshape(grp)
        o_ref[pl.ds(r0, CONF_ROWS), :] = acc.reshape(CONF_ROWS, BRANCH_W) + b_ref[...]


def _conf_finish(o_ref, g_ref, beta_ref):
    acc = o_ref[...]
    mu = jnp.mean(acc, axis=-1, keepdims=True)
    cen = acc - mu
    var = jnp.mean(cen * cen, axis=-1, keepdims=True)
    y = cen * lax.rsqrt(var + EPS) * g_ref[...] + beta_ref[...]
    o_ref[...] = y * jax.nn.sigmoid(y)


def _rel_bucket_np(rel):
    nb = REL_BUCKETS // 2
    max_exact = nb // 2
    ret = np.where(rel > 0, nb, 0)
    n = np.abs(rel)
    ratio = np.log(np.maximum(n, 1).astype(np.float32) / np.float32(max_exact)) / np.float32(
        math.log(REL_MAX_DIST / max_exact))
    large = max_exact + (ratio * np.float32(nb - max_exact)).astype(np.int32)
    large = np.minimum(large, nb - 1)
    return ret + np.where(n < max_exact, n, large)


def _bias_buckets():
    r = np.arange(ATT_TK)[:, None]
    c = np.arange(ATT_TQ)[None, :]
    span = ATT_NEAR - 1
    out = np.zeros((ATT_TK // ATT_TQ, 2 * span + 1, ATT_TK, ATT_TQ), np.int32)
    for e in range(ATT_TK // ATT_TQ):
        for dj in range(-span, span + 1):
            out[e, dj + span] = _rel_bucket_np(ATT_TK * dj + r - ATT_TQ * e - c)
        assert (out[e, :span - 1] == REL_BUCKETS // 2 - 1).all() and (out[e, span + 2:] == REL_BUCKETS - 1).all()
    return out


def _attn_kernel(cfar_ref, q_ref, k_ref, vt_ref, bias_ref, lq1_ref, lk1_ref, lq2_ref, lk2_ref, sg_ref,
                 o_ref, qt_ref, s_ref, smax_ref, m_ref, acc_ref, *, nk, lam_init):
    h = pl.program_id(1)
    qi = pl.program_id(2)
    qt = q_ref[...].astype(F32).T
    feat = lax.broadcasted_iota(jnp.int32, qt.shape, 0)
    qt_ref[0] = jnp.where(feat < DA_HEAD_DIM, qt, 0.0).astype(BF16)
    qt_ref[1] = jnp.where(feat >= DA_HEAD_DIM, qt, 0.0).astype(BF16)
    m_ref[...] = jnp.full(m_ref.shape, NEG_BIG, F32)
    acc_ref[...] = jnp.zeros(acc_ref.shape, F32)
    j0 = qi // (ATT_TK // ATT_TQ)
    e = qi % (ATT_TK // ATT_TQ)
    c_left = cfar_ref[h, 0]
    c_right = cfar_ref[h, 1]

    def scores(j, slot, near_kind=None):
        kblk = k_ref[pl.ds(pl.multiple_of(j * ATT_TK, ATT_TK), ATT_TK), :]
        for mm in range(2):
            s = jnp.dot(kblk, qt_ref[mm], preferred_element_type=F32)
            if near_kind is not None:
                s = s + bias_ref[e, near_kind]
            s_ref[slot, mm] = s
            smax_ref[slot, mm] = jnp.max(s, axis=0, keepdims=True)

    def softmax_values(j, slot, off):
        vblk = vt_ref[j]
        for mm in range(2):
            m_prev = m_ref[mm]
            m_new = jnp.maximum(m_prev, smax_ref[slot, mm] + off)
            p = jnp.exp2(s_ref[slot, mm] - (m_new - off)).astype(BF16)
            acc_ref[mm] = (jnp.exp2(m_prev - m_new) * acc_ref[mm]
                           + jnp.dot(vblk, p, preferred_element_type=F32))
            m_ref[mm] = m_new

    near_lo = jnp.clip(j0 - 1, 0, nk - ATT_NEAR)
    n_far = nk - ATT_NEAR

    def far_chunk(t):
        t = jnp.minimum(t, n_far - 1)
        return jnp.where(t < near_lo, t, t + ATT_NEAR)

    def far_offset(j):
        return jnp.where(j < j0, c_left, c_right)

    scores(near_lo, 0, near_lo - j0 + ATT_NEAR - 1)
    for i in range(ATT_NEAR):
        j = near_lo + i
        if i + 1 < ATT_NEAR:
            scores(j + 1, (i + 1) % 2, j + 1 - j0 + ATT_NEAR - 1)
        else:
            scores(far_chunk(0), (i + 1) % 2)
        softmax_values(j, i % 2, 0.0)

    def body(i, carry):
        for u in range(ATT_UNROLL):
            t = ATT_UNROLL * i + u
            cur = u % 2
            scores(far_chunk(t + 1), 1 - cur)
            j = far_chunk(t)
            softmax_values(j, cur, far_offset(j))
        return carry

    lax.fori_loop(0, n_far // ATT_UNROLL, body, 0)

    lam = (jnp.exp(jnp.sum(lq1_ref[...] * lk1_ref[...], keepdims=True))
           - jnp.exp(jnp.sum(lq2_ref[...] * lk2_ref[...], keepdims=True)) + lam_init)
    a0 = acc_ref[0]
    a1 = acc_ref[1]
    o = (a0[:DA_V_DIM] / a0[DA_V_DIM:DA_V_DIM + 1]
         - lam * (a1[:DA_V_DIM] / a1[DA_V_DIM:DA_V_DIM + 1]))
    ms = jnp.mean(o * o, axis=0, keepdims=True)
    y = o * lax.rsqrt(ms + EPS) * sg_ref[...] * (1.0 - lam_init)
    o_ref[...] = y.T


def _attention(qk, vt, bias, cfar, lq1, lk1, lq2, lk2, sg, lam_init):
    B, L, _ = qk.shape
    nk = L // ATT_TK
    assert (nk - ATT_NEAR) % ATT_UNROLL == 0 and ATT_UNROLL % 2 == 0 and ATT_NEAR % 2 == 0 and nk > ATT_NEAR
    vec = pl.BlockSpec((1, DA_HEAD_DIM), lambda b, h, i: (0, 0))
    return pl.pallas_call(
        functools.partial(_attn_kernel, nk=nk, lam_init=lam_init),
        grid=(B, DA_HEADS, L // ATT_TQ),
        in_specs=[pl.BlockSpec(memory_space=pltpu.SMEM),
                  pl.BlockSpec((None, ATT_TQ, LANES), lambda b, h, i: (b, i, h)),
                  pl.BlockSpec((None, L, LANES), lambda b, h, i: (b, 0, DA_HEADS + h)),
                  pl.BlockSpec((None, None, nk, V_ROWS, ATT_TK), lambda b, h, i: (b, h, 0, 0, 0)),
                  pl.BlockSpec((None, ATT_TK // ATT_TQ, 2 * ATT_NEAR - 1, ATT_TK, ATT_TQ), lambda b, h, i: (h, 0, 0, 0, 0)),
                  vec, vec, vec, vec,
                  pl.BlockSpec((DA_V_DIM, 1), lambda b, h, i: (0, 0))],
        out_specs=pl.BlockSpec((None, ATT_TQ, LANES), lambda b, h, i: (b, i, h)),
        out_shape=jax.ShapeDtypeStruct((B, L, BRANCH_W), F32),
        scratch_shapes=[pltpu.VMEM((2, LANES, ATT_TQ), BF16),
                        pltpu.VMEM((2, 2, ATT_TK, ATT_TQ), F32),
                        pltpu.VMEM((2, 2, 1, ATT_TQ), F32),
                        pltpu.VMEM((2, 1, ATT_TQ), F32),
                        pltpu.VMEM((2, V_ROWS, ATT_TQ), F32)],
        compiler_params=_cparams(("parallel", "parallel", "arbitrary"), 52),
        name="diff_attention",
    )(cfar, qk, qk, vt, bias, lq1, lk1, lq2, lk2, sg)


def _spectrum_rows(n2):
    return -(-(n2 // 2 + 1) // (2 * SUBLANES)) * (2 * SUBLANES)


def _fft_consts(n2):
    n1 = FFT_N1
    n = n1 * n2
    f2 = np.exp(-2j * np.pi * np.outer(np.arange(n2), np.arange(n2)) / n2)
    tw = np.exp(-2j * np.pi * np.outer(np.arange(n2), np.arange(n1)) / n)
    f1 = np.exp(-2j * np.pi * np.outer(np.arange(n1), np.arange(n1)) / n1)
    nr = n2 // 2 + 1
    pad = ((0, _spectrum_rows(n2) - nr), (0, 0))
    c = {}
    c["f2_full"] = np.concatenate([np.pad(f2.real[:nr], pad), np.pad(f2.imag[:nr], pad)], 0)
    c["f2_half"] = c["f2_full"][:, : n2 // 2]
    c["tre"], c["tim"] = np.pad(tw.real[:nr], pad), np.pad(tw.imag[:nr], pad)
    c["g_top"] = np.concatenate([f1.real, f1.imag], 1)
    c["g_bot"] = np.concatenate([-f1.imag, f1.real], 1)
    c["gi_top"] = np.concatenate([f1.real, -f1.imag], 1)
    c["gi_bot"] = np.concatenate([f1.imag, f1.real], 1)
    wgt = np.full((nr, 1), 2.0)
    wgt[0] = wgt[nr - 1] = 1.0
    c["fin_re"] = np.pad(wgt * f2.real[:nr], pad).T[: n2 // 2]
    c["fin_im"] = np.pad(wgt * f2.imag[:nr], pad).T[: n2 // 2]
    out = {}
    for k, v in c.items():
        out[k] = jnp.asarray(v, F32 if k in ("tre", "tim") else BF16)
    return out


def _filter_td_kernel(emb_ref, t_ref, w1_ref, b1_ref, f0_ref, w2_ref, b2_ref, f1_ref, w3_ref, ad_ref, o_ref):
    emb = emb_ref[...].astype(BF16)
    h = jnp.dot(w1_ref[...], emb, preferred_element_type=F32) + b1_ref[...]
    h = jnp.sin(f0_ref[...] * h)
    h = jnp.dot(w2_ref[...], h.astype(BF16), preferred_element_type=F32) + b2_ref[...]
    h = jnp.sin(f1_ref[...] * h)
    h = jnp.dot(w3_ref[...], h.astype(BF16), preferred_element_type=F32)
    decay = jnp.exp(-(ad_ref[...] * t_ref[...]))
    o_ref[...] = h * decay


def _filter_td(emb_t, t_full, w1t, b1, f0, w2t, b2, f1, w3t, absdelta):
    n = emb_t.shape[1]
    pt = 2048
    nch = n // pt
    col = lambda r: pl.BlockSpec((r, 1), lambda i: (0, 0))
    return pl.pallas_call(
        _filter_td_kernel,
        grid=(nch,),
        in_specs=[pl.BlockSpec((HY_EMB_PAD, pt), lambda i: (0, i)),
                  pl.BlockSpec((1, pt), lambda i: (0, i)),
                  pl.BlockSpec((HY_ORDER, HY_EMB_PAD), lambda i: (0, 0)), col(HY_ORDER), col(HY_ORDER),
                  pl.BlockSpec((HY_ORDER, HY_ORDER), lambda i: (0, 0)), col(HY_ORDER), col(HY_ORDER),
                  pl.BlockSpec((None, BRANCH_W, HY_ORDER), lambda i: (i // (nch // 2), 0, 0)),
                  col(BRANCH_W)],
        out_specs=pl.BlockSpec((BRANCH_W, pt), lambda i: (0, i)),
        out_shape=jax.ShapeDtypeStruct((BRANCH_W, n), F32),
        compiler_params=_cparams(("parallel",), 32),
        name="hyena_filter_time",
    )(emb_t, t_full, w1t, b1, f0, w2t, b2, f1, w3t, absdelta)


def _cmul(are, aim, bre, bim):
    return are * bre - aim * bim, are * bim + aim * bre


def _filter_fft_kernel(f_ref, f2_ref, tre_ref, tim_ref, gt_ref, gb_ref, o_ref, bre_ref, bim_ref, *, cc, n2):
    n_total = FFT_N1 * n2
    nr = _spectrum_rows(n2)
    row = lax.broadcasted_iota(jnp.int32, (n2, FFT_N1), 0)
    lane = lax.broadcasted_iota(jnp.int32, (n2, FFT_N1), 1)
    is_lag0_backward = (row == n2 // 2) & (lane == 0)

    def per_channel(c, carry):
        f = f_ref[c]
        s = jnp.sum(jnp.abs(f), keepdims=True)
        f = jnp.where(is_lag0_backward, 0.0, f) * (1.0 / ((s + EPS) * n_total))
        a = jnp.dot(f2_ref[...], f.astype(BF16), preferred_element_type=F32)
        bre, bim = _cmul(a[:nr], a[nr:], tre_ref[...], tim_ref[...])
        bre_ref[c] = bre.astype(BF16)
        bim_ref[c] = bim.astype(BF16)
        return carry

    lax.fori_loop(0, cc, per_channel, 0, unroll=min(HY_CH_UNROLL, cc))
    bre = bre_ref[...].reshape(cc * nr, FFT_N1)
    bim = bim_ref[...].reshape(cc * nr, FFT_N1)
    o_ref[...] = (jnp.dot(bre, gt_ref[...], preferred_element_type=F32)
                  + jnp.dot(bim, gb_ref[...], preferred_element_type=F32))


def _filter_fft(filt, consts, n2, cc):
    C = filt.shape[0]
    nr = _spectrum_rows(n2)
    full = lambda a: pl.BlockSpec(a.shape, lambda i: (0,) * a.ndim)
    mats = [consts["f2_full"], consts["tre"], consts["tim"], consts["g_top"], consts["g_bot"]]
    return pl.pallas_call(
        functools.partial(_filter_fft_kernel, cc=cc, n2=n2),
        grid=(C // cc,),
        in_specs=[pl.BlockSpec((cc, n2, FFT_N1), lambda i: (i, 0, 0))] + [full(a) for a in mats],
        out_specs=pl.BlockSpec((cc * nr, 2 * FFT_N1), lambda i: (i, 0)),
        out_shape=jax.ShapeDtypeStruct((C * nr, 2 * FFT_N1), F32),
        scratch_shapes=[pltpu.VMEM((cc, nr, FFT_N1), BF16), pltpu.VMEM((cc, nr, FFT_N1), BF16)],
        compiler_params=_cparams(("parallel",), 40),
        name="hyena_filter_fft",
    )(filt, *mats)


def _shift_prev(x, row, lane):
    a = pltpu.roll(x, 1, axis=1)
    b = pltpu.roll(a, 1, axis=0)
    return jnp.where(lane == 0, jnp.where(row == 0, 0.0, b), a)


def _shift_next(x, row, lane, rows):
    a = pltpu.roll(x, FFT_N1 - 1, axis=1)
    b = pltpu.roll(a, rows - 1, axis=0)
    return jnp.where(lane == FFT_N1 - 1, jnp.where(row == rows - 1, 0.0, b), a)


def _hyena_kernel(sw_ref, sb_ref, d_ref, x0_ref, x1_ref, v_ref, hf_ref, f2_ref, tre_ref, tim_ref,
                  gt_ref, gb_ref, git_ref, gib_ref, fre_ref, fim_ref, o_ref,
                  z_ref, x0c_ref, bre_ref, bim_ref, c_ref, *, cc, n2):
    rows = n2 // 2
    nr = _spectrum_rows(n2)
    ch0 = pl.program_id(1) * cc
    row = lax.broadcasted_iota(jnp.int32, (rows, FFT_N1), 0)
    lane = lax.broadcasted_iota(jnp.int32, (rows, FFT_N1), 1)

    def short_conv(x, ch):
        return (sw_ref[0, ch] * _shift_prev(x, row, lane) + sw_ref[1, ch] * x
                + sw_ref[2, ch] * _shift_next(x, row, lane, rows) + sb_ref[ch])

    def forward(c, carry):
        ch = ch0 + c
        x0c_ref[c] = short_conv(x0_ref[c], ch)
        z = short_conv(x1_ref[c], BRANCH_W + ch) * short_conv(v_ref[c], 2 * BRANCH_W + ch)
        z_ref[c] = z
        a = jnp.dot(f2_ref[...], z.astype(BF16), preferred_element_type=F32)
        bre, bim = _cmul(a[:nr], a[nr:], tre_ref[...], tim_ref[...])
        bre_ref[c] = bre.astype(BF16)
        bim_ref[c] = bim.astype(BF16)
        return carry

    lax.fori_loop(0, cc, forward, 0, unroll=min(HY_CH_UNROLL, cc))

    bre = bre_ref[...].reshape(cc * nr, FFT_N1)
    bim = bim_ref[...].reshape(cc * nr, FFT_N1)
    x = (jnp.dot(bre, gt_ref[...], preferred_element_type=F32)
         + jnp.dot(bim, gb_ref[...], preferred_element_type=F32))
    hf = hf_ref[...]
    yre, yim = _cmul(x[:, :FFT_N1], x[:, FFT_N1:], hf[:, :FFT_N1], hf[:, FFT_N1:])
    cfull = (jnp.dot(yre.astype(BF16), git_ref[...], preferred_element_type=F32)
             + jnp.dot(yim.astype(BF16), gib_ref[...], preferred_element_type=F32))
    c_ref[...] = cfull.reshape(cc, nr, 2 * FFT_N1)

    def backward(c, carry):
        cm = c_ref[c]
        cre, cim = cm[:, :FFT_N1], cm[:, FFT_N1:]
        tre, tim = tre_ref[...], tim_ref[...]
        dre = cre * tre + cim * tim
        dim = cim * tre - cre * tim
        y = (jnp.dot(fre_ref[...], dre.astype(BF16), preferred_element_type=F32)
             + jnp.dot(fim_ref[...], dim.astype(BF16), preferred_element_type=F32))
        o_ref[c] = x0c_ref[c] * (y + z_ref[c] * d_ref[ch0 + c])
        return carry

    lax.fori_loop(0, cc, backward, 0, unroll=min(HY_CH_UNROLL, cc))


def _hyena(u, hf, consts, short_w, short_b, d_bias, n2, cc):
    B = u.shape[0]
    rows = n2 // 2
    nr = _spectrum_rows(n2)
    C = BRANCH_W
    nblk = C // cc
    smem = pl.BlockSpec(memory_space=pltpu.SMEM)
    full = lambda a: pl.BlockSpec(a.shape, lambda b, i: (0,) * a.ndim)
    mats = [consts[k] for k in ("f2_half", "tre", "tim", "g_top", "g_bot", "gi_top", "gi_bot", "fin_re", "fin_im")]
    sec = lambda s: pl.BlockSpec((None, cc, rows, FFT_N1), lambda b, i: (b, s * nblk + i, 0, 0))
    return pl.pallas_call(
        functools.partial(_hyena_kernel, cc=cc, n2=n2),
        grid=(B, nblk),
        in_specs=[smem, smem, smem, sec(0), sec(1), sec(2),
                  pl.BlockSpec((cc * nr, 2 * FFT_N1), lambda b, i: (i, 0))] + [full(a) for a in mats],
        out_specs=pl.BlockSpec((None, cc, rows, FFT_N1), lambda b, i: (b, i, 0, 0)),
        out_shape=jax.ShapeDtypeStruct((B, C, rows, FFT_N1), F32),
        scratch_shapes=[pltpu.VMEM((cc, rows, FFT_N1), F32), pltpu.VMEM((cc, rows, FFT_N1), F32),
                        pltpu.VMEM((cc, nr, FFT_N1), BF16), pltpu.VMEM((cc, nr, FFT_N1), BF16),
                        pltpu.VMEM((cc, nr, 2 * FFT_N1), F32)],
        compiler_params=_cparams(("parallel", "arbitrary"), 40),
        name="hyena_mixer",
    )(short_w, short_b, d_bias, u, u, u, hf, *mats)


def _hyena_tables(L):
    t = jnp.linspace(0.0, 1.0, L, dtype=F32)[:, None]
    w = 2.0 * math.pi * jnp.arange(L, dtype=F32)[:, None] / L
    bands = jnp.linspace(1e-4, HY_BANDS - 1, HY_BANDS, dtype=F32)[None, :]
    emb = jnp.concatenate([t, jnp.cos(bands * w), -jnp.sin(bands * w)], axis=-1)
    two_sided = lambda a: jnp.concatenate([a, a[:1], a[:0:-1]], axis=0)
    emb_full = jnp.pad(two_sided(emb), ((0, 0), (0, HY_EMB_PAD - HY_EMB)))
    t_full = two_sided(t).T
    return emb_full.T, t_full


def _merge_kernel(x_ref, pa_ref, pa_prev_ref, pa_next_ref, yb_ref, yct_ref, cf_ref, cf_prev_ref, cf_next_ref,
                  g_ref, ws_ref, wm_ref, wb_ref, wo_ref, fg_ref, pw_ref, psc_ref, dw_ref, db_ref, lg_ref, lb_ref,
                  o_ref, pext_ref, cext_ref, hs_ref, ya_ref, yd_ref, *, final, tm, L):
    _fill_ext(pext_ref, pa_ref, pa_prev_ref, pa_next_ref, tm, L)
    _fill_ext(cext_ref, cf_ref, cf_prev_ref, cf_next_ref, tm, L)
    _pool_body(pext_ref, pw_ref, psc_ref, ya_ref, tm, L)
    _conf_prepare(cext_ref, hs_ref, tm)
    x = x_ref[...]
    ms = jnp.mean(x * x, axis=-1, keepdims=True)
    h = (x * lax.rsqrt(ms + EPS) * g_ref[...]).astype(BF16)
    mixed = None
    n_blocks = tm // CONF_ROWS
    for step, n in enumerate((1, 2, 0, 3)):
        if step < 3:
            _conf_taps(hs_ref, dw_ref, db_ref, yd_ref, range(step * n_blocks // 3, (step + 1) * n_blocks // 3))
        else:
            _conf_finish(yd_ref, lg_ref, lb_ref)
        if n == 0:
            y = ya_ref[...]
        elif n == 1:
            y = yb_ref[...]
        elif n == 2:
            y = yct_ref[...].T
        else:
            y = yd_ref[...]
        sg = jnp.dot(h, ws_ref[:, n * BRANCH_W:(n + 1) * BRANCH_W], preferred_element_type=F32)
        u = (y * (sg * jax.nn.sigmoid(sg))).astype(BF16)
        proj = jnp.dot(u, wb_ref[n], preferred_element_type=F32)
        merge = jax.nn.sigmoid(jnp.dot(h, wm_ref[:, n * D_MODEL:(n + 1) * D_MODEL], preferred_element_type=F32))
        mixed = merge * proj if mixed is None else mixed + merge * proj
    out = x + jnp.dot(mixed.astype(BF16), wo_ref[...], preferred_element_type=F32)
    if final:
        ms = jnp.mean(out * out, axis=-1, keepdims=True)
        out = out * lax.rsqrt(ms + EPS) * fg_ref[...]
    o_ref[...] = out


def _merge(x2, pool_in, yb, yct, cf, p, fg, L, final):
    T, D = x2.shape
    tm = MERGE_TILE
    nl = L // tm
    W = BRANCH_W
    tok = lambda c: pl.BlockSpec((tm, c), lambda i: (i, 0))
    const = lambda a: pl.BlockSpec(a.shape, lambda i: (0,) * a.ndim, pipeline_mode=pl.Buffered(1))
    consts = [p["g"], p["w_silu"], p["w_merge"], p["w_branch"], p["w_out"], fg, p["pool_w"], p["pool_scale"],
              p["dw_w"], p["dw_b"], p["ln_g"], p["ln_b"]]
    return pl.pallas_call(
        functools.partial(_merge_kernel, final=final, tm=tm, L=L),
        grid=(T // tm,),
        in_specs=[tok(D)] + _halo_specs(tm, W, T) + [tok(W), pl.BlockSpec((None, W, tm), lambda i: (i // nl, 0, i % nl))]
        + _halo_specs(tm, 2 * W, T) + [const(a) for a in consts],
        out_specs=tok(D),
        out_shape=jax.ShapeDtypeStruct((T, D), F32),
        scratch_shapes=[pltpu.VMEM((tm + 2 * HALO, W), F32), pltpu.VMEM((tm + 2 * HALO, 2 * W), F32),
                        pltpu.VMEM((SUBLANES, tm + 2 * HALO, W), F32), pltpu.VMEM((tm, W), F32),
                        pltpu.VMEM((tm, W), F32)],
        compiler_params=_cparams(("parallel",), 56),
        name="gated_merge",
    )(x2, pool_in, pool_in, pool_in, yb, yct, cf, cf, cf, *consts)


def _layer_weights(l, norm_g, w_in, pool_w, pool_scale, subln_g, rel_bias, hy_short_w, hy_short_b, hy_w1, hy_b1,
                   hy_freq, hy_w2, hy_b2, hy_w3, hy_d, cf_dw_w, cf_dw_b, cf_ln_g, cf_ln_b, w_branch, w_out):
    w = w_in[l].astype(BF16)
    W = BRANCH_W
    p = {}
    p["g"] = norm_g[l][None, :]
    p["w_tok"] = jnp.concatenate([w[:, OFF_POOL:OFF_V], w[:, OFF_CF:OFF_SILU]], axis=1)
    p["w_chan"] = w[:, OFF_V:OFF_CF].T
    p["w_silu"] = w[:, OFF_SILU:OFF_MERGE]
    p["w_merge"] = w[:, OFF_MERGE:]
    p["w_branch"] = w_branch[l].astype(BF16)
    p["w_out"] = w_out[l].astype(BF16)
    p["pool_w"] = pool_w[l].astype(BF16)
    p["pool_scale"] = pool_scale[l][None, :]
    p["subln_g"] = subln_g[l][:, None]
    buckets = jnp.asarray(_bias_buckets())[None]
    table_t = rel_bias.astype(F32).T * LOG2E
    bias = jnp.zeros((DA_HEADS,) + buckets.shape[1:], F32)
    for b in range(REL_BUCKETS):
        bias = jnp.where(buckets == b, table_t[:, b][:, None, None, None, None], bias)
    p["bias"] = bias
    p["cfar"] = jnp.stack([table_t[:, REL_BUCKETS // 2 - 1], table_t[:, REL_BUCKETS - 1]], axis=1)
    p["short_w"] = hy_short_w[l]
    p["short_b"] = hy_short_b[l]
    p["hy_d"] = hy_d[l]
    p["w1t"] = jnp.pad(hy_w1[l].T, ((0, 0), (0, HY_EMB_PAD - HY_EMB))).astype(BF16)
    p["b1"] = hy_b1[l][:, None]
    p["f0"] = hy_freq[l, 0][:, None]
    p["w2t"] = hy_w2[l].T.astype(BF16)
    p["b2"] = hy_b2[l][:, None]
    p["f1"] = hy_freq[l, 1][:, None]
    p["w3t"] = hy_w3[l].T.reshape(2, W, HY_ORDER).astype(BF16)
    max_decay = math.log(HY_DECAY_TARGET) / HY_FAST
    min_decay = math.log(HY_DECAY_TARGET) / HY_SLOW
    p["absdelta"] = jnp.abs(jnp.linspace(min_decay, max_decay, W, dtype=F32))[:, None]
    p["dw_w"] = jnp.broadcast_to(cf_dw_w[l][:, None, :], (CF_WIDTH, SUBLANES, BRANCH_W))
    p["dw_b"] = cf_dw_b[l][None, :]
    p["ln_g"] = cf_ln_g[l][None, :]
    p["ln_b"] = cf_ln_b[l][None, :]
    return p


def _hyena_chunk(n2):
    return max(1, 1024 // n2)


def _filter_spectrum(p, L):
    n2 = 2 * L // FFT_N1
    consts = _fft_consts(n2)
    emb_t, t_full = _hyena_tables(L)
    filt = _filter_td(emb_t, t_full, p["w1t"], p["b1"], p["f0"], p["w2t"], p["b2"], p["f1"], p["w3t"],
                      p["absdelta"])
    hf = _filter_fft(filt.reshape(BRANCH_W, n2, FFT_N1), consts, n2, _hyena_chunk(n2))
    return hf, consts, n2


def _layer(x, p, l, lq1, lk1, lq2, lk2, final_g, final):
    B, L, D = x.shape
    T = B * L
    x2 = x.reshape(T, D)
    g = p["g"]
    lam_init = 0.8 - 0.6 * math.exp(-0.3 * l)

    pool_in, qk, cf, vt, hyt = _project(x2, g, p["w_tok"], p["w_chan"], B, L)
    qk = qk.reshape(B, L, 2 * BRANCH_W)

    yb = _attention(qk, vt, p["bias"], p["cfar"], lq1[l][None, :], lk1[l][None, :], lq2[l][None, :],
                    lk2[l][None, :], p["subln_g"], lam_init).reshape(T, BRANCH_W)
    hf, consts, n2 = _filter_spectrum(p, L)
    yct = _hyena(hyt.reshape(B, 3 * BRANCH_W, n2 // 2, FFT_N1), hf, consts, p["short_w"], p["short_b"], p["hy_d"],
                 n2, _hyena_chunk(n2)).reshape(B, BRANCH_W, L)
    out = _merge(x2, pool_in, yb, yct, cf, p, final_g[None, :], L, final)
    return out.reshape(B, L, D)


def kernel(x_prompt, x_sample, norm_g, w_in, pool_w, pool_scale, lambda_q1, lambda_k1, lambda_q2, lambda_k2, subln_g, rel_bias, hy_short_w, hy_short_b, hy_w1, hy_b1, hy_freq, hy_w2, hy_b2, hy_w3, hy_d, cf_dw_w, cf_dw_b, cf_ln_g, cf_ln_b, w_branch, w_out, final_g):
    depth = norm_g.shape[0]
    hp, hs = x_prompt, x_sample
    for l in range(depth):
        p = _layer_weights(l, norm_g, w_in, pool_w, pool_scale, subln_g, rel_bias, hy_short_w, hy_short_b, hy_w1,
                           hy_b1, hy_freq, hy_w2, hy_b2, hy_w3, hy_d, cf_dw_w, cf_dw_b, cf_ln_g, cf_ln_b, w_branch,
                           w_out)
        final = l == depth - 1
        hp = _layer(hp, p, l, lambda_q1, lambda_k1, lambda_q2, lambda_k2, final_g, final)
        hs = _layer(hs, p, l, lambda_q1, lambda_k1, lambda_q2, lambda_k2, final_g, final)
    return (hp, hs)
```
